```python
import jax, jax.numpy as jnp
from jax import lax
import numpy as np

D_MODEL = 1024
BATCH = 16
SEQ = 2048
DEPTH = 2

CTX_LEN = 256
GRID_W = 64
HEAD_DIM = 64
ROPE_THETA = 10000.0
NORM_EPS = 1e-6
Q_BLOCK = 128
NEG_INF = -1e30
MIX_HEADS = D_MODEL // HEAD_DIM

A_HEADS = MIX_HEADS // 2
A_KV_HEADS = 2
A_GROUP = A_HEADS // A_KV_HEADS
A_Q = A_HEADS * HEAD_DIM
A_KV = A_KV_HEADS * HEAD_DIM
B_HEADS = MIX_HEADS // 2
B_Q_RANK = D_MODEL // 4
B_KV_RANK = D_MODEL // 8
B_NOPE_DIM = 64
B_ROPE_DIM = 32
B_V_DIM = 64
B_QK_DIM = B_NOPE_DIM + B_ROPE_DIM
EVEN_SPLITS = (A_Q, A_Q + A_KV, A_Q + 2 * A_KV, A_Q + 2 * A_KV + B_Q_RANK,
               A_Q + 2 * A_KV + B_Q_RANK + B_KV_RANK)
EVEN_IN = EVEN_SPLITS[-1] + B_ROPE_DIM
EVEN_OUT = A_HEADS * HEAD_DIM + B_HEADS * B_V_DIM

C_GROUPS = MIX_HEADS // 2
C_WIDTH = C_GROUPS * HEAD_DIM
C_CHUNK = 128
D_HEADS = MIX_HEADS // 2
D_KV_HEADS = 2
D_GROUP = D_HEADS // D_KV_HEADS
D_Q = D_HEADS * HEAD_DIM
D_KV = D_KV_HEADS * HEAD_DIM
WINDOW = 128
ODD_SPLITS = (C_WIDTH, 2 * C_WIDTH, 2 * C_WIDTH + D_Q, 2 * C_WIDTH + D_Q + D_KV)
ODD_IN = ODD_SPLITS[-1] + D_KV
ODD_OUT = C_WIDTH + D_Q

D_FF = 2816
CONV_W = 3

N_EVEN = (DEPTH + 1) // 2
N_ODD = DEPTH // 2

kernel_name = "hybrid_diffusion_gqa_mla_gmlp_swa_convffn"


def rms_norm(x, g):
    xf = x.astype(jnp.float32)
    y = xf * lax.rsqrt(jnp.mean(xf * xf, axis=-1, keepdims=True) + NORM_EPS)
    return (y * g.astype(jnp.float32)).astype(x.dtype)


def layer_norm(x, g, b):
    xf = x.astype(jnp.float32)
    mu = jnp.mean(xf, axis=-1, keepdims=True)
    xc = xf - mu
    y = xc * lax.rsqrt(jnp.mean(xc * xc, axis=-1, keepdims=True) + NORM_EPS)
    return (y * g.astype(jnp.float32) + b.astype(jnp.float32)).astype(x.dtype)


def modulate(x, shift, scale):
    return x * (1 + scale) + shift


def axial_rope_tables(n_tok, dim):
    n_rows = n_tok // GRID_W
    rows = jnp.repeat(jnp.arange(n_rows), GRID_W).astype(jnp.float32)
    cols = jnp.tile(jnp.arange(GRID_W), n_rows).astype(jnp.float32)
    quarter = dim // 4
    inv_freq = ROPE_THETA ** (-jnp.arange(quarter, dtype=jnp.float32) / quarter)
    ang = jnp.concatenate([rows[:, None] * inv_freq, cols[:, None] * inv_freq], axis=-1)
    return jnp.cos(ang), jnp.sin(ang)


def apply_rope(x, cos, sin):
    half = x.shape[-1] // 2
    x1, x2 = x[..., :half], x[..., half:]
    cos = cos.astype(x.dtype)
    sin = sin.astype(x.dtype)
    return jnp.concatenate([x1 * cos - x2 * sin, x1 * sin + x2 * cos], axis=-1)


def to_q_heads(x, n_kv, group, d):
    b, t, _ = x.shape
    return x.reshape(b, t, n_kv, group, d).transpose(0, 2, 3, 1, 4)


def to_kv_heads(x, n_kv, d):
    b, t, _ = x.shape
    return x.reshape(b, t, n_kv, d).transpose(0, 2, 1, 3)


def merge_heads(o):
    b, hkv, g, t, d = o.shape
    return o.transpose(0, 3, 1, 2, 4).reshape(b, t, hkv * g * d)


def attend(q, k, v, scale, mask=None, sink=None):
    s = jnp.einsum('bhgqd,bhkd->bhgqk', q, k).astype(jnp.float32) * scale
    if mask is not None:
        s = jnp.where(mask, s, NEG_INF)
    if sink is not None:
        hkv, g = q.shape[1], q.shape[2]
        sink_col = jnp.broadcast_to(sink.astype(jnp.float32).reshape(1, hkv, g, 1, 1), s.shape[:-1] + (1,))
        s = jnp.concatenate([s, sink_col], axis=-1)
    p = jax.nn.softmax(s, axis=-1)
    if sink is not None:
        p = p[..., :-1]
    return jnp.einsum('bhgqk,bhkd->bhgqd', p.astype(v.dtype), v)


def blocked_attend(q, k, v, scale):
    b, hkv, g, s, dk = q.shape
    nb = s // Q_BLOCK
    qb = jnp.moveaxis(q.reshape(b, hkv, g, nb, Q_BLOCK, dk), 3, 0)
    out = lax.map(lambda qi: attend(qi, k, v, scale), qb)
    return jnp.moveaxis(out, 0, 3).reshape(b, hkv, g, s, v.shape[-1])


def window_attend(q, k, v, kc, vc, sink, scale):
    b, hkv, g, s, d = q.shape
    nb = s // Q_BLOCK
    n_ctx = kc.shape[2]
    pad = ((0, 0), (0, 0), (Q_BLOCK, Q_BLOCK), (0, 0))
    kp, vp = jnp.pad(k, pad), jnp.pad(v, pad)
    qb = jnp.moveaxis(q.reshape(b, hkv, g, nb, Q_BLOCK, d), 3, 0)
    offset = jnp.arange(Q_BLOCK)[:, None] + Q_BLOCK - jnp.arange(3 * Q_BLOCK)[None, :]
    band = jnp.abs(offset) <= WINDOW
    ctx_ok = jnp.ones((Q_BLOCK, n_ctx), dtype=bool)

    def one_block(args):
        n, qi = args
        start = n * Q_BLOCK
        kw = lax.dynamic_slice_in_dim(kp, start, 3 * Q_BLOCK, axis=2)
        vw = lax.dynamic_slice_in_dim(vp, start, 3 * Q_BLOCK, axis=2)
        kpos = start - Q_BLOCK + jnp.arange(3 * Q_BLOCK)
        local_ok = band & ((kpos >= 0) & (kpos < s))[None, :]
        mask = jnp.concatenate([ctx_ok, local_ok], axis=-1)
        return attend(qi, jnp.concatenate([kc, kw], axis=2), jnp.concatenate([vc, vw], axis=2),
                      scale, mask=mask, sink=sink)

    out = lax.map(one_block, (jnp.arange(nb), qb))
    return jnp.moveaxis(out, 0, 3).reshape(b, hkv, g, s, d)


def even_project(z, w_in, qa_g, ka_g, qlat_g, w_q_up, kvlat_g, w_kv_up, rope):
    h = z @ w_in
    qa, ka, va, cq, ckv, kr = jnp.split(h, EVEN_SPLITS, axis=-1)
    qa = rms_norm(to_q_heads(qa, A_KV_HEADS, A_GROUP, HEAD_DIM), qa_g)
    ka = rms_norm(to_kv_heads(ka, A_KV_HEADS, HEAD_DIM), ka_g)
    va = to_kv_heads(va, A_KV_HEADS, HEAD_DIM)
    qb = to_q_heads(rms_norm(cq, qlat_g) @ w_q_up, B_HEADS, 1, B_QK_DIM)
    kvb = to_kv_heads(rms_norm(ckv, kvlat_g) @ w_kv_up, B_HEADS, B_NOPE_DIM + B_V_DIM)
    kb_nope, vb = kvb[..., :B_NOPE_DIM], kvb[..., B_NOPE_DIM:]
    kr = kr[:, None]
    if rope is not None:
        cos_h, sin_h, cos_r, sin_r = rope
        qa = apply_rope(qa, cos_h, sin_h)
        ka = apply_rope(ka, cos_h, sin_h)
        qb = jnp.concatenate([qb[..., :B_NOPE_DIM], apply_rope(qb[..., B_NOPE_DIM:], cos_r, sin_r)], axis=-1)
        kr = apply_rope(kr, cos_r, sin_r)
    kb = jnp.concatenate([kb_nope, jnp.broadcast_to(kr, kb_nope.shape[:-1] + (B_ROPE_DIM,))], axis=-1)
    return qa, ka, va, qb, kb, vb


def even_mixer(zc, zl, w_in, qa_g, ka_g, qlat_g, w_q_up, kvlat_g, w_kv_up, w_out, rope, need_ctx):
    qa_c, ka_c, va_c, qb_c, kb_c, vb_c = even_project(zc, w_in, qa_g, ka_g, qlat_g, w_q_up, kvlat_g, w_kv_up, None)
    qa_l, ka_l, va_l, qb_l, kb_l, vb_l = even_project(zl, w_in, qa_g, ka_g, qlat_g, w_q_up, kvlat_g, w_kv_up, rope)
    scale_a = HEAD_DIM ** -0.5
    scale_b = B_QK_DIM ** -0.5
    ya = blocked_attend(qa_l, jnp.concatenate([ka_c, ka_l], axis=2), jnp.concatenate([va_c, va_l], axis=2), scale_a)
    yb = blocked_attend(qb_l, jnp.concatenate([kb_c, kb_l], axis=2), jnp.concatenate([vb_c, vb_l], axis=2), scale_b)
    yl = jnp.concatenate([merge_heads(ya), merge_heads(yb)], axis=-1) @ w_out
    yc = None
    if need_ctx:
        yac = attend(qa_c, ka_c, va_c, scale_a)
        ybc = attend(qb_c, kb_c, vb_c, scale_b)
        yc = jnp.concatenate([merge_heads(yac), merge_heads(ybc)], axis=-1) @ w_out
    return yc, yl


def chunk_sgu(u, v, ln_g, ln_b, w_s, b_s):
    bsz, t, _ = v.shape
    nc = t // C_CHUNK
    v = layer_norm(jax.nn.gelu(v), ln_g, ln_b)
    vg = v.reshape(bsz, nc, C_CHUNK, C_GROUPS, HEAD_DIM)
    mixed = jnp.einsum('gpq,bnqgd->bnpgd', w_s, vg) + b_s.T[:, :, None]
    return jax.nn.gelu(u) * mixed.reshape(bsz, t, C_WIDTH)


def odd_project(z, w_in, rope):
    h = z @ w_in
    u, v, qd, kd, vd = jnp.split(h, ODD_SPLITS, axis=-1)
    qd = to_q_heads(qd, D_KV_HEADS, D_GROUP, HEAD_DIM)
    kd = to_kv_heads(kd, D_KV_HEADS, HEAD_DIM)
    vd = to_kv_heads(vd, D_KV_HEADS, HEAD_DIM)
    if rope is not None:
        cos_h, sin_h = rope
        qd = apply_rope(qd, cos_h, sin_h)
        kd = apply_rope(kd, cos_h, sin_h)
    return u, v, qd, kd, vd


def odd_mixer(zc, zl, w_in, ln_g, ln_b, w_s, b_s, sink, w_out, rope, need_ctx):
    u_c, v_c, qd_c, kd_c, vd_c = odd_project(zc, w_in, None)
    u_l, v_l, qd_l, kd_l, vd_l = odd_project(zl, w_in, rope)
    scale = HEAD_DIM ** -0.5
    yc_l = chunk_sgu(u_l, v_l, ln_g, ln_b, w_s, b_s)
    yd_l = window_attend(qd_l, kd_l, vd_l, kd_c, vd_c, sink, scale)
    yl = jnp.concatenate([yc_l, merge_heads(yd_l)], axis=-1) @ w_out
    yc = None
    if need_ctx:
        yc_c = chunk_sgu(u_c, v_c, ln_g, ln_b, w_s, b_s)
        yd_c = attend(qd_c, kd_c, vd_c, scale, sink=sink)
        yc = jnp.concatenate([yc_c, merge_heads(yd_c)], axis=-1) @ w_out
    return yc, yl


def conv_ffn(z, w_up, conv_w, conv_b, w_down):
    h = z @ w_up
    t = h.shape[1]
    half = CONV_W // 2
    hp = jnp.pad(h, ((0, 0), (half, half), (0, 0)))
    h = sum(hp[:, i:i + t] * conv_w[i] for i in range(CONV_W)) + conv_b
    a, g = jnp.split(h, 2, axis=-1)
    return (jax.nn.silu(g) * a) @ w_down


def setup_inputs(seed: int = 0) -> dict:
    key = jax.random.key(seed)
    ks = iter(jax.random.split(key, 40))

    def nrm(shape, scale):
        return jax.random.normal(next(ks), shape, jnp.float32) * scale

    def gain(shape):
        return 1.0 + nrm(shape, 0.02)

    d = D_MODEL
    return {
        "x": nrm((BATCH, SEQ, d), 1.0),
        "c": nrm((BATCH, d), 1.0),
        "ctx": nrm((BATCH, CTX_LEN, d), 1.0),
        "c_ctx": nrm((d,), 1.0),
        "mod_w": nrm((DEPTH, d, 6 * d), 0.5 * d ** -0.5),
        "mod_b": nrm((DEPTH, 6 * d), 0.02),
        "norm1_g": gain((DEPTH, d)),
        "norm2_g": gain((DEPTH, d)),
        "ev_w_in": nrm((N_EVEN, d, EVEN_IN), d ** -0.5),
        "ev_qa_g": gain((N_EVEN, HEAD_DIM)),
        "ev_ka_g": gain((N_EVEN, HEAD_DIM)),
        "ev_qlat_g": gain((N_EVEN, B_Q_RANK)),
        "ev_w_q_up": nrm((N_EVEN, B_Q_RANK, B_HEADS * B_QK_DIM), B_Q_RANK ** -0.5),
        "ev_kvlat_g": gain((N_EVEN, B_KV_RANK)),
        "ev_w_kv_up": nrm((N_EVEN, B_KV_RANK, B_HEADS * (B_NOPE_DIM + B_V_DIM)), B_KV_RANK ** -0.5),
        "ev_w_out": nrm((N_EVEN, EVEN_OUT, d), EVEN_OUT ** -0.5),
        "od_w_in": nrm((N_ODD, d, ODD_IN), d ** -0.5),
        "od_ln_g": gain((N_ODD, C_WIDTH)),
        "od_ln_b": nrm((N_ODD, C_WIDTH), 0.02),
        "od_sgu_w": nrm((N_ODD, C_GROUPS, C_CHUNK, C_CHUNK), C_CHUNK ** -0.5),
        "od_sgu_b": 1.0 + nrm((N_ODD, C_GROUPS, C_CHUNK), 0.02),
        "od_sink": nrm((N_ODD, D_HEADS), 0.5),
        "od_w_out": nrm((N_ODD, ODD_OUT, d), ODD_OUT ** -0.5),
        "ffn_up": nrm((DEPTH, d, 2 * D_FF), d ** -0.5),
        "ffn_conv_w": nrm((DEPTH, CONV_W, 2 * D_FF), CONV_W ** -0.5),
        "ffn_conv_b": nrm((DEPTH, 2 * D_FF), 0.02),
        "ffn_down": nrm((DEPTH, D_FF, d), D_FF ** -0.5),
        "final_g": gain((d,)),
    }


def reference(x, c, ctx, c_ctx, mod_w, mod_b, norm1_g, norm2_g,
              ev_w_in, ev_qa_g, ev_ka_g, ev_qlat_g, ev_w_q_up, ev_kvlat_g, ev_w_kv_up, ev_w_out,
              od_w_in, od_ln_g, od_ln_b, od_sgu_w, od_sgu_b, od_sink, od_w_out,
              ffn_up, ffn_conv_w, ffn_conv_b, ffn_down, final_g):
    n_tok = x.shape[1]
    cos_h, sin_h = axial_rope_tables(n_tok, HEAD_DIM)
    cos_r, sin_r = axial_rope_tables(n_tok, B_ROPE_DIM)
    hl, hc = x, ctx
    for layer in range(DEPTH):
        need_ctx = layer != DEPTH - 1
        w_m, b_m = mod_w[layer], mod_b[layer]
        mod_l = [m[:, None, :] for m in jnp.split(jax.nn.silu(c) @ w_m + b_m, 6, axis=-1)]
        mod_c = jnp.split(jax.nn.silu(c_ctx) @ w_m + b_m, 6, axis=-1)
        zl = modulate(rms_norm(hl, norm1_g[layer]), mod_l[0], mod_l[1])
        zc = modulate(rms_norm(hc, norm1_g[layer]), mod_c[0], mod_c[1])
        if layer % 2 == 0:
            e = layer // 2
            yc, yl = even_mixer(zc, zl, ev_w_in[e], ev_qa_g[e], ev_ka_g[e], ev_qlat_g[e], ev_w_q_up[e],
                                ev_kvlat_g[e], ev_w_kv_up[e], ev_w_out[e],
                                (cos_h, sin_h, cos_r, sin_r), need_ctx)
        else:
            o = layer // 2
            yc, yl = odd_mixer(zc, zl, od_w_in[o], od_ln_g[o], od_ln_b[o], od_sgu_w[o], od_sgu_b[o],
                               od_sink[o], od_w_out[o], (cos_h, sin_h), need_ctx)
        hl = hl + mod_l[2] * yl
        hl = hl + mod_l[5] * conv_ffn(modulate(rms_norm(hl, norm2_g[layer]), mod_l[3], mod_l[4]),
                                      ffn_up[layer], ffn_conv_w[layer], ffn_conv_b[layer], ffn_down[layer])
        if need_ctx:
            hc = hc + mod_c[2] * yc
            hc = hc + mod_c[5] * conv_ffn(modulate(rms_norm(hc, norm2_g[layer]), mod_c[3], mod_c[4]),
                                          ffn_up[layer], ffn_conv_w[layer], ffn_conv_b[layer], ffn_down[layer])
    return rms_norm(hl, final_g)
```

```python
import functools

import jax
import jax.numpy as jnp
from jax import lax
from jax.experimental import pallas as pl
from jax.experimental.pallas import tpu as pltpu

F32 = jnp.float32
BF16 = jnp.bfloat16

D_MODEL = 1024
HEAD_DIM = 64
GRID_W = 64
ROPE_THETA = 10000.0
NORM_EPS = 1e-6
NEG_INF = -1e30
N_HEADS_HALF = 8
KV_HEADS = 2
GQA_GROUP = N_HEADS_HALF // KV_HEADS
A_Q = N_HEADS_HALF * HEAD_DIM
A_KV = KV_HEADS * HEAD_DIM
B_Q_RANK = 256
B_KV_RANK = 128
B_NOPE = 64
B_ROPE = 32
B_V = 64
B_QK = B_NOPE + B_ROPE
C_WIDTH = 512
C_GROUPS = 8
C_CHUNK = 128
WINDOW = 128
D_FF = 2816
LANES = 128
SUBLANES = 8
TILE = 256
FF_CHUNK = 256
VMEM_LIMIT = 56 * 1024 * 1024


def _dot(a, b):
    return jnp.dot(a, b, preferred_element_type=F32)


def _dot_nt(a, b):
    return lax.dot_general(a, b, (((1,), (1,)), ((), ())), preferred_element_type=F32)


def _dot_tn(a, b):
    return lax.dot_general(a, b, (((0,), (0,)), ((), ())), preferred_element_type=F32)


def _lane_tile(v, width):
    return jnp.concatenate([v] * (width // LANES), axis=1)


def _norm_modulate(h, g, shift, scale):
    ms = jnp.mean(h * h, axis=-1, keepdims=True)
    y = h * lax.rsqrt(ms + NORM_EPS) * g
    return y * (1.0 + scale) + shift


def _rope_rows(x, cos, sin):
    half = cos.shape[0]
    x1 = x[..., :half, :]
    x2 = x[..., half:, :]
    return jnp.concatenate([x1 * cos - x2 * sin, x1 * sin + x2 * cos], axis=-2)


def _rope_lanes(x, cos, sin_signed, half, first_half_mask):
    up = pltpu.roll(x, LANES - half, 1)
    down = pltpu.roll(x, half, 1)
    return x * cos + jnp.where(first_half_mask, up, down) * sin_signed


def _mod_kernel(c_ref, w_ref, b_ref, o_ref):
    c = c_ref[...]
    a = c * jax.nn.sigmoid(c)
    o_ref[...] = jnp.dot(a, w_ref[...], preferred_element_type=F32,
                         precision=lax.Precision.HIGHEST) + b_ref[...]


def _modulation(cc, mod_w, mod_b):
    depth, d, n = mod_w.shape
    rows = cc.shape[0]
    bn = 1024
    return pl.pallas_call(
        _mod_kernel,
        grid=(depth, n // bn),
        in_specs=[
            pl.BlockSpec((rows, d), lambda l, j: (0, 0)),
            pl.BlockSpec((None, d, bn), lambda l, j: (l, 0, j)),
            pl.BlockSpec((None, 1, bn), lambda l, j: (l, 0, j)),
        ],
        out_specs=pl.BlockSpec((None, rows, bn), lambda l, j: (l, 0, j)),
        out_shape=jax.ShapeDtypeStruct((depth, rows, n), F32),
        compiler_params=pltpu.CompilerParams(dimension_semantics=("arbitrary", "arbitrary")),
        name="modulation",
    )(cc, mod_w, mod_b.reshape(depth, 1, n))


def _even_proj_kernel(x_ref, ctx_ref, mod_ref, g1_ref, w_row_ref, w_t_ref, wq_up_ref, wk_up_ref, wv_up_ref,
                      qa_g_ref, ka_g_ref, qlat_g_ref, kvlat_gc_ref, kvlat_gr_ref,
                      cos_ht_ref, sin_ht_ref, cos_rt_ref, sin_rt_ref,
                      cos_k_ref, sin_k_ref, cos_kr_ref, sin_kr_ref,
                      qa_ref, ka_ref, va_ref, qb_ref, kb_ref, vb_ref):
    t = pl.program_id(1)
    tile = x_ref.shape[0]
    h = jnp.where(t == 0, ctx_ref[...], x_ref[...])
    z = _norm_modulate(h, g1_ref[...], mod_ref[0:1, :], mod_ref[1:2, :]).astype(BF16)
    hr = _dot(z, w_row_ref[...])
    ht = _dot_nt(w_t_ref[...], z)
    lane = lax.broadcasted_iota(jnp.int32, (tile, LANES), 1)

    qa = ht[0:A_Q].reshape(N_HEADS_HALF, HEAD_DIM, tile)
    r = lax.rsqrt(jnp.mean(qa * qa, axis=1, keepdims=True) + NORM_EPS)
    qa = qa * r * _lane_tile(qa_g_ref[...], tile)
    qa = _rope_rows(qa, cos_ht_ref[...], sin_ht_ref[...])
    qa_ref[...] = qa.reshape(A_Q, tile).astype(BF16)

    ka = hr[:, 0:A_KV]
    sq = ka * ka
    head0 = lane < HEAD_DIM
    ss0 = jnp.sum(jnp.where(head0, sq, 0.0), axis=-1, keepdims=True)
    ss1 = jnp.sum(jnp.where(head0, 0.0, sq), axis=-1, keepdims=True)
    r = lax.rsqrt(jnp.where(head0, ss0, ss1) * (1.0 / HEAD_DIM) + NORM_EPS)
    ka = ka * r * ka_g_ref[...]
    ka = _rope_lanes(ka, cos_k_ref[...], sin_k_ref[...], HEAD_DIM // 2,
                     (lane & (HEAD_DIM - 1)) < HEAD_DIM // 2)
    ka_ref[...] = ka.astype(BF16)

    va_ref[...] = ht[A_Q:A_Q + A_KV].astype(BF16)

    o = A_Q + A_KV
    cq = ht[o:o + B_Q_RANK]
    r = lax.rsqrt(jnp.mean(cq * cq, axis=0, keepdims=True) + NORM_EPS)
    cq = (cq * r * _lane_tile(qlat_g_ref[...], tile)).astype(BF16)
    qb = _dot(wq_up_ref[...], cq).reshape(N_HEADS_HALF, LANES, tile)
    q_rope = _rope_rows(qb[:, B_NOPE:B_QK, :], cos_rt_ref[...], sin_rt_ref[...])
    qb = jnp.concatenate([qb[:, :B_NOPE, :], q_rope, qb[:, B_QK:, :]], axis=1) * (B_QK ** -0.5)
    qb_ref[...] = qb.reshape(N_HEADS_HALF * LANES, tile).astype(BF16)

    ckv = hr[:, A_KV:A_KV + B_KV_RANK]
    r = lax.rsqrt(jnp.mean(ckv * ckv, axis=-1, keepdims=True) + NORM_EPS)
    ckv = (ckv * r * kvlat_gr_ref[...]).astype(BF16)
    k_nope = _dot(ckv, wk_up_ref[...])
    kr = hr[:, A_KV + B_KV_RANK:A_KV + B_KV_RANK + LANES]
    kr = _rope_lanes(kr, cos_kr_ref[...], sin_kr_ref[...], B_ROPE // 2, lane < B_NOPE + B_ROPE // 2)
    for hd in range(N_HEADS_HALF):
        kb_ref[:, hd * LANES:(hd + 1) * LANES] = (k_nope[:, hd * LANES:(hd + 1) * LANES] + kr).astype(BF16)

    o = A_Q + A_KV + B_Q_RANK
    ckvt = ht[o:o + B_KV_RANK]
    r = lax.rsqrt(jnp.mean(ckvt * ckvt, axis=0, keepdims=True) + NORM_EPS)
    ckvt = (ckvt * r * _lane_tile(kvlat_gc_ref[...], tile)).astype(BF16)
    vb_ref[...] = _dot(wv_up_ref[...], ckvt).astype(BF16)


def _const_spec(shape):
    nd = len(shape)
    return pl.BlockSpec(shape, lambda b, t, _n=nd: (0,) * _n)


def _tile_spec(rows, cols):
    return pl.BlockSpec((None, None, rows, cols), lambda b, t: (b, t, 0, 0))


def _even_proj(x, ctx, mod, g1, wts, tabs):
    bsz, seq, d = x.shape
    n_tiles = seq // TILE + 1
    n_rows = mod.shape[0]

    def mod_map(b, t):
        return (jnp.where(t == 0, n_rows - 1, b), 0, 0)

    in_specs = [
        pl.BlockSpec((None, TILE, d), lambda b, t: (b, jnp.maximum(t - 1, 0), 0)),
        pl.BlockSpec((None, TILE, d), lambda b, t: (b, 0, 0)),
        pl.BlockSpec((None, 6, d), mod_map),
        _const_spec(g1.shape),
    ]
    in_specs += [_const_spec(w.shape) for w in wts]
    in_specs += [pl.BlockSpec((None,) + tb.shape[1:], lambda b, t: (t, 0, 0)) for tb in tabs]
    outs = [
        (A_Q, TILE), (TILE, A_KV), (A_KV, TILE),
        (N_HEADS_HALF * LANES, TILE), (TILE, N_HEADS_HALF * LANES), (N_HEADS_HALF * B_V, TILE),
    ]
    return pl.pallas_call(
        _even_proj_kernel,
        grid=(bsz, n_tiles),
        in_specs=in_specs,
        out_specs=[_tile_spec(r, c) for r, c in outs],
        out_shape=[jax.ShapeDtypeStruct((bsz, n_tiles, r, c), BF16) for r, c in outs],
        compiler_params=pltpu.CompilerParams(dimension_semantics=("parallel", "arbitrary"),
                                             vmem_limit_bytes=VMEM_LIMIT),
        name="even_proj",
    )(x, ctx, mod, g1, *wts, *tabs)


def _softmax_attend(score_chunk, value_chunk, n_chunks, s_ref, tile, dv):
    groups = TILE // SUBLANES

    def pass1(c, m):
        s = score_chunk(c)
        s_ref[c] = s
        return jnp.maximum(m, jnp.max(s.reshape(groups, SUBLANES, tile), axis=0))

    m = lax.fori_loop(0, n_chunks, pass1, jnp.full((SUBLANES, tile), -jnp.inf, F32))
    m = jnp.max(m, axis=0, keepdims=True)

    def pass2(c, carry):
        l, acc = carry
        p = jnp.exp(s_ref[c] - m)
        l = l + jnp.sum(p.reshape(groups, SUBLANES, tile), axis=0)
        acc = acc + _dot(value_chunk(c), p.astype(BF16))
        return l, acc

    l, acc = lax.fori_loop(0, n_chunks, pass2,
                           (jnp.zeros((SUBLANES, tile), F32), jnp.zeros((dv, tile), F32)))
    l = jnp.sum(l, axis=0, keepdims=True)
    return acc * (1.0 / l)


def _even_attn_kernel(qa_ref, qb_ref, ka_ref, va_ref, kb_ref, vb_ref, y_ref, s_ref):
    t = pl.program_id(1)
    n_tiles = ka_ref.shape[0]
    tile = qa_ref.shape[1]
    n_chunks = jnp.where(t == 0, 1, n_tiles)
    zeros = jnp.zeros((HEAD_DIM, tile), BF16)
    for hd in range(N_HEADS_HALF):
        kv = hd // GQA_GROUP
        q = qa_ref[hd * HEAD_DIM:(hd + 1) * HEAD_DIM, :]
        q = jnp.concatenate([q, zeros] if kv == 0 else [zeros, q], axis=0)
        out = _softmax_attend(
            lambda c, q=q: _dot(ka_ref[c], q),
            lambda c, kv=kv: va_ref[c, kv * HEAD_DIM:(kv + 1) * HEAD_DIM, :],
            n_chunks, s_ref, tile, HEAD_DIM)
        y_ref[hd * HEAD_DIM:(hd + 1) * HEAD_DIM, :] = out.astype(BF16)
    for hd in range(N_HEADS_HALF):
        q = qb_ref[hd * LANES:(hd + 1) * LANES, :]
        out = _softmax_attend(
            lambda c, q=q, hd=hd: _dot(kb_ref[c, :, hd * LANES:(hd + 1) * LANES], q),
            lambda c, hd=hd: vb_ref[c, hd * B_V:(hd + 1) * B_V, :],
            n_chunks, s_ref, tile, B_V)
        y_ref[A_Q + hd * B_V:A_Q + (hd + 1) * B_V, :] = out.astype(BF16)


def _even_attn(qa, ka, va, qb, kb, vb):
    bsz, n_tiles = qa.shape[:2]

    def batch_spec(arr):
        return pl.BlockSpec((None,) + arr.shape[1:], lambda b, t: (b, 0, 0, 0))

    return pl.pallas_call(
        _even_attn_kernel,
        grid=(bsz, n_tiles),
        in_specs=[_tile_spec(*qa.shape[2:]), _tile_spec(*qb.shape[2:]),
                  batch_spec(ka), batch_spec(va), batch_spec(kb), batch_spec(vb)],
        out_specs=_tile_spec(D_MODEL, TILE),
        out_shape=jax.ShapeDtypeStruct((bsz, n_tiles, D_MODEL, TILE), BF16),
        scratch_shapes=[pltpu.VMEM((n_tiles, TILE, TILE), F32)],
        compiler_params=pltpu.CompilerParams(dimension_semantics=("parallel", "arbitrary"),
                                             vmem_limit_bytes=VMEM_LIMIT),
        name="even_attn",
    )(qa, qb, ka, va, kb, vb)


def _out_proj_kernel(two_inputs, *refs):
    if two_inputs:
        x_ref, ctx_ref, mod_ref, y0_ref, y1_ref, w0_ref, w1_ref, o_ref = refs
        h = jnp.where(pl.program_id(1) == 0, ctx_ref[...], x_ref[...])
    else:
        x_ref, mod_ref, y0_ref, y1_ref, w0_ref, w1_ref, o_ref = refs
        h = x_ref[...]
    y = _dot_tn(y0_ref[...], w0_ref[...]) + _dot_tn(y1_ref[...], w1_ref[...])
    o_ref[...] = h + mod_ref[2:3, :] * y


def _out_proj(h_in, mod, y0, y1, w_out, tile_offset, n_out_tiles):
    two_inputs = isinstance(h_in, tuple)
    (y0, off0), (y1, off1) = y0, y1
    bsz = y0.shape[0]
    d = w_out.shape[1]
    n_rows = mod.shape[0]
    half = w_out.shape[0] // 2

    def mod_map(b, t):
        return (jnp.where(t + tile_offset == 0, n_rows - 1, b), 0, 0)

    if two_inputs:
        x, ctx = h_in
        h_specs = [pl.BlockSpec((None, TILE, d), lambda b, t: (b, jnp.maximum(t - 1, 0), 0)),
                   pl.BlockSpec((None, TILE, d), lambda b, t: (b, 0, 0))]
        h_args = [x, ctx]
    else:
        h_specs = [pl.BlockSpec((None, TILE, d), lambda b, t: (b, t + tile_offset, 0))]
        h_args = [h_in]

    def y_spec(arr, off, part):
        return pl.BlockSpec((None, None, half, TILE), lambda b, t: (b, t + off, part, 0))

    same = y0 is y1
    return pl.pallas_call(
        functools.partial(_out_proj_kernel, two_inputs),
        grid=(bsz, n_out_tiles),
        in_specs=h_specs + [
            pl.BlockSpec((None, 6, d), mod_map),
            y_spec(y0, off0, 0), y_spec(y1, off1, 1 if same else 0),
            pl.BlockSpec((half, d), lambda b, t: (0, 0)),
            pl.BlockSpec((half, d), lambda b, t: (1, 0)),
        ],
        out_specs=pl.BlockSpec((None, TILE, d), lambda b, t: (b, t, 0)),
        out_shape=jax.ShapeDtypeStruct((bsz, n_out_tiles * TILE, d), F32),
        compiler_params=pltpu.CompilerParams(dimension_semantics=("parallel", "arbitrary"),
                                             vmem_limit_bytes=VMEM_LIMIT),
        name="out_proj",
    )(*h_args, mod, y0, y1, w_out, w_out)


def _conv_ffn_kernel(has_ctx_tile, final_norm, *refs):
    if final_norm:
        (h_ref, prev_ref, next_ref, mod_ref, g2_ref, w_up_ref, cw_ref, cb_ref, w_down_ref, gf_ref,
         o_ref, hs_ref, acc_ref) = refs
    else:
        (h_ref, prev_ref, next_ref, mod_ref, g2_ref, w_up_ref, cw_ref, cb_ref, w_down_ref,
         o_ref, hs_ref, acc_ref) = refs
    t = pl.program_id(1)
    last_tile = pl.num_programs(1) - 1
    tile = h_ref.shape[0]
    n_chunks = w_up_ref.shape[0]
    g2 = g2_ref[...]
    shift, scale = mod_ref[3:4, :], mod_ref[4:5, :]
    if has_ctx_tile:
        has_prev = jnp.logical_and(t != 0, t != 1).astype(F32)
        has_next = jnp.logical_and(t != 0, t != last_tile).astype(F32)
    else:
        has_prev = (t != 0).astype(F32)
        has_next = (t != last_tile).astype(F32)
    h = h_ref[...]
    z = jnp.concatenate([
        _norm_modulate(prev_ref[...], g2, shift, scale) * has_prev,
        _norm_modulate(h, g2, shift, scale),
        _norm_modulate(next_ref[...], g2, shift, scale) * has_next,
    ], axis=0).astype(BF16)

    acc_ref[...] = jnp.zeros_like(acc_ref)

    def chunk(j, carry):
        hs_ref[...] = _dot(z, w_up_ref[j])
        cw = cw_ref[j]
        conv = (hs_ref[SUBLANES - 1:SUBLANES - 1 + tile, :] * cw[0:1, :]
                + hs_ref[SUBLANES:SUBLANES + tile, :] * cw[1:2, :]
                + hs_ref[SUBLANES + 1:SUBLANES + 1 + tile, :] * cw[2:3, :]
                + cb_ref[j])
        a = conv[:, :FF_CHUNK]
        g = conv[:, FF_CHUNK:]
        act = (g * jax.nn.sigmoid(g) * a).astype(BF16)
        acc_ref[...] += _dot(act, w_down_ref[j])
        return carry

    lax.fori_loop(0, n_chunks, chunk, 0)
    out = h + mod_ref[5:6, :] * acc_ref[...]
    if final_norm:
        ms = jnp.mean(out * out, axis=-1, keepdims=True)
        out = out * lax.rsqrt(ms + NORM_EPS) * gf_ref[...]
    o_ref[...] = out


def _conv_ffn(h, mod, g2, w_up, conv_w, conv_b, w_down, has_ctx_tile, final_g=None):
    bsz, t_total, d = h.shape
    n_rows = mod.shape[0]
    n_out_tiles = t_total // TILE
    rows_per_tile = TILE // SUBLANES
    last_block = t_total // SUBLANES - 1
    final_norm = final_g is not None

    def mod_map(b, t):
        return (jnp.where(t == 0, n_rows - 1, b) if has_ctx_tile else b, 0, 0)

    in_specs = [
        pl.BlockSpec((None, TILE, d), lambda b, t: (b, t, 0)),
        pl.BlockSpec((None, SUBLANES, d), lambda b, t: (b, jnp.maximum(t * rows_per_tile - 1, 0), 0)),
        pl.BlockSpec((None, SUBLANES, d),
                     lambda b, t: (b, jnp.minimum((t + 1) * rows_per_tile, last_block), 0)),
        pl.BlockSpec((None, 6, d), mod_map),
        _const_spec(g2.shape), _const_spec(w_up.shape), _const_spec(conv_w.shape),
        _const_spec(conv_b.shape), _const_spec(w_down.shape),
    ]
    args = [h, h, h, mod, g2, w_up, conv_w, conv_b, w_down]
    if final_norm:
        in_specs.append(_const_spec(final_g.shape))
        args.append(final_g)
    return pl.pallas_call(
        functools.partial(_conv_ffn_kernel, has_ctx_tile, final_norm),
        grid=(bsz, n_out_tiles),
        in_specs=in_specs,
        out_specs=pl.BlockSpec((None, TILE, d), lambda b, t: (b, t, 0)),
        out_shape=jax.ShapeDtypeStruct((bsz, n_out_tiles * TILE, d), F32),
        scratch_shapes=[pltpu.VMEM((TILE + 2 * SUBLANES, 2 * FF_CHUNK), F32),
                        pltpu.VMEM((TILE, d), F32)],
        compiler_params=pltpu.CompilerParams(dimension_semantics=("parallel", "arbitrary"),
                                             vmem_limit_bytes=VMEM_LIMIT),
        name="conv_ffn",
    )(*args)


def _gelu(x):
    return 0.5 * x * (1.0 + jnp.tanh(0.7978845608028654 * (x + 0.044715 * (x * x * x))))


def _odd_proj_kernel(h_ref, mod_ref, g1_ref, w_row_ref, w_t_ref, ln_g_ref, ln_b_ref, ws_ref, bs_ref,
                     cos_ht_ref, sin_ht_ref, cos_k_ref, sin_k_ref,
                     yc_ref, qd_ref, kd_ref, vd_ref):
    tile = h_ref.shape[0]
    z = _norm_modulate(h_ref[...], g1_ref[...], mod_ref[0:1, :], mod_ref[1:2, :]).astype(BF16)
    hr = _dot(z, w_row_ref[...])
    ht = _dot_nt(w_t_ref[...], z)
    lane = lax.broadcasted_iota(jnp.int32, (tile, LANES), 1)

    v = _gelu(ht[C_WIDTH:2 * C_WIDTH])
    mu = jnp.mean(v, axis=0, keepdims=True)
    vc = v - mu
    var = jnp.mean(vc * vc, axis=0, keepdims=True)
    vn = vc * lax.rsqrt(var + NORM_EPS) * _lane_tile(ln_g_ref[...], tile) + _lane_tile(ln_b_ref[...], tile)
    vn = vn.astype(BF16)
    gw = C_WIDTH // C_GROUPS
    for g in range(C_GROUPS):
        for n in range(tile // C_CHUNK):
            cols = slice(n * C_CHUNK, (n + 1) * C_CHUNK)
            rows = slice(g * gw, (g + 1) * gw)
            mixed = _dot(vn[rows, cols], ws_ref[g]) + bs_ref[g]
            yc_ref[rows, cols] = (_gelu(ht[rows, cols]) * mixed).astype(BF16)

    o = 2 * C_WIDTH
    qd = ht[o:o + A_Q].reshape(N_HEADS_HALF, HEAD_DIM, tile)
    qd = _rope_rows(qd, cos_ht_ref[...], sin_ht_ref[...]) * (HEAD_DIM ** -0.5)
    qd_ref[...] = qd.reshape(A_Q, tile).astype(BF16)
    kd = _rope_lanes(hr, cos_k_ref[...], sin_k_ref[...], HEAD_DIM // 2,
                     (lane & (HEAD_DIM - 1)) < HEAD_DIM // 2).astype(BF16)
    vd = ht[o + A_Q:o + A_Q + A_KV].astype(BF16)
    for n in range(tile // C_CHUNK):
        kd_ref[n] = kd[n * C_CHUNK:(n + 1) * C_CHUNK, :]
        vd_ref[n] = vd[:, n * C_CHUNK:(n + 1) * C_CHUNK]


def _odd_proj(h, mod, g1, wts, tabs):
    bsz, t_total, d = h.shape
    n_tiles = t_total // TILE
    n_rows = mod.shape[0]
    sub = TILE // C_CHUNK

    def mod_map(b, t):
        return (jnp.where(t == 0, n_rows - 1, b), 0, 0)

    in_specs = [
        pl.BlockSpec((None, TILE, d), lambda b, t: (b, t, 0)),
        pl.BlockSpec((None, 6, d), mod_map),
        _const_spec(g1.shape),
    ]
    in_specs += [_const_spec(w.shape) for w in wts]
    in_specs += [pl.BlockSpec((None,) + tb.shape[1:], lambda b, t: (t, 0, 0)) for tb in tabs]
    chunk_spec = pl.BlockSpec((None, sub, C_CHUNK, C_CHUNK), lambda b, t: (b, t, 0, 0))
    return pl.pallas_call(
        _odd_proj_kernel,
        grid=(bsz, n_tiles),
        in_specs=in_specs,
        out_specs=[_tile_spec(C_WIDTH, TILE), _tile_spec(A_Q, TILE), chunk_spec, chunk_spec],
        out_shape=[jax.ShapeDtypeStruct((bsz, n_tiles, C_WIDTH, TILE), BF16),
                   jax.ShapeDtypeStruct((bsz, n_tiles, A_Q, TILE), BF16),
                   jax.ShapeDtypeStruct((bsz, n_tiles * sub, C_CHUNK, A_KV), BF16),
                   jax.ShapeDtypeStruct((bsz, n_tiles * sub, A_KV, C_CHUNK), BF16)],
        compiler_params=pltpu.CompilerParams(dimension_semantics=("parallel", "arbitrary"),
                                             vmem_limit_bytes=VMEM_LIMIT),
        name="odd_proj",
    )(h, mod, g1, *wts, *tabs)


def _window_attn_kernel(sink_ref, qd_ref, kd_ref, vd_ref, y_ref):
    t = pl.program_id(1)
    tile = qd_ref.shape[1]
    n_kchunks = kd_ref.shape[0]
    ctx_chunks = TILE // C_CHUNK
    sub = tile // C_CHUNK
    win_chunks = sub + 2
    lat_chunks = n_kchunks - ctx_chunks
    w0 = jnp.clip(t * sub - 1, 0, lat_chunks - win_chunks)
    n_keys = (ctx_chunks + win_chunks) * C_CHUNK

    k_ctx = kd_ref[0:ctx_chunks].reshape(ctx_chunks * C_CHUNK, A_KV)
    k_win = kd_ref[pl.ds(ctx_chunks + w0, win_chunks)].reshape(win_chunks * C_CHUNK, A_KV)
    keys = jnp.concatenate([k_ctx, k_win], axis=0)
    key_pos = w0 * C_CHUNK + lax.broadcasted_iota(jnp.int32, (win_chunks * C_CHUNK, tile), 0)
    qry_pos = t * tile + lax.broadcasted_iota(jnp.int32, (win_chunks * C_CHUNK, tile), 1)
    band = jnp.abs(qry_pos - key_pos) <= WINDOW
    zeros = jnp.zeros((HEAD_DIM, tile), BF16)
    groups = n_keys // SUBLANES
    for hd in range(N_HEADS_HALF):
        kv = hd // GQA_GROUP
        q = qd_ref[hd * HEAD_DIM:(hd + 1) * HEAD_DIM, :]
        q = jnp.concatenate([q, zeros] if kv == 0 else [zeros, q], axis=0)
        s = _dot(keys, q)
        s = jnp.concatenate([s[:ctx_chunks * C_CHUNK],
                             jnp.where(band, s[ctx_chunks * C_CHUNK:], NEG_INF)], axis=0)
        sink = sink_ref[hd]
        m = jnp.max(s.reshape(groups, SUBLANES, tile), axis=0)
        m = jnp.maximum(jnp.max(m, axis=0, keepdims=True), sink)
        p = jnp.exp(s - m)
        l = jnp.sum(p.reshape(groups, SUBLANES, tile), axis=0)
        l = jnp.sum(l, axis=0, keepdims=True) + jnp.exp(sink - m)
        p = p.astype(BF16)
        rows = slice(kv * HEAD_DIM, (kv + 1) * HEAD_DIM)
        acc = jnp.zeros((HEAD_DIM, tile), F32)
        for c in range(ctx_chunks + win_chunks):
            vchunk = vd_ref[c, rows, :] if c < ctx_chunks else vd_ref[w0 + c, rows, :]
            acc = acc + _dot(vchunk, p[c * C_CHUNK:(c + 1) * C_CHUNK, :])
        y_ref[hd * HEAD_DIM:(hd + 1) * HEAD_DIM, :] = (acc * (1.0 / l)).astype(BF16)


def _window_attn(sink, qd, kd, vd, n_lat_tiles):
    bsz = qd.shape[0]

    def batch_spec(arr):
        return pl.BlockSpec((None,) + arr.shape[1:], lambda b, t: (b, 0, 0, 0))

    return pl.pallas_call(
        _window_attn_kernel,
        grid=(bsz, n_lat_tiles),
        in_specs=[pl.BlockSpec(memory_space=pltpu.SMEM),
                  pl.BlockSpec((None, None, A_Q, TILE), lambda b, t: (b, t + 1, 0, 0)),
                  batch_spec(kd), batch_spec(vd)],
        out_specs=_tile_spec(A_Q, TILE),
        out_shape=jax.ShapeDtypeStruct((bsz, n_lat_tiles, A_Q, TILE), BF16),
        compiler_params=pltpu.CompilerParams(dimension_semantics=("parallel", "arbitrary"),
                                             vmem_limit_bytes=VMEM_LIMIT),
        name="window_attn",
    )(sink, qd, kd, vd)


def _rope_tables(seq, ctx_len):
    def base(dim):
        n_rows = seq // GRID_W
        rows = jnp.repeat(jnp.arange(n_rows), GRID_W).astype(F32)
        cols = jnp.tile(jnp.arange(GRID_W), n_rows).astype(F32)
        quarter = dim // 4
        inv_freq = ROPE_THETA ** (-jnp.arange(quarter, dtype=F32) / quarter)
        ang = jnp.concatenate([rows[:, None] * inv_freq, cols[:, None] * inv_freq], axis=-1)
        cos = jnp.concatenate([jnp.ones((ctx_len, dim // 2), F32), jnp.cos(ang)], axis=0)
        sin = jnp.concatenate([jnp.zeros((ctx_len, dim // 2), F32), jnp.sin(ang)], axis=0)
        return cos, sin

    n_tiles = (seq + ctx_len) // TILE

    def feature_major(tb):
        return tb.reshape(n_tiles, TILE, tb.shape[1]).transpose(0, 2, 1)

    def token_major(tb):
        return tb.reshape(n_tiles, TILE, LANES)

    cos_h, sin_h = base(HEAD_DIM)
    cos_r, sin_r = base(B_ROPE)
    cos_k = jnp.tile(cos_h, (1, LANES // (HEAD_DIM // 2)))
    sin_k = jnp.tile(jnp.concatenate([-sin_h, sin_h], axis=1), (1, LANES // HEAD_DIM))
    t_total = seq + ctx_len
    pad_l = jnp.zeros((t_total, B_NOPE), F32)
    pad_r = jnp.zeros((t_total, LANES - B_QK), F32)
    cos_kr = jnp.concatenate([pad_l, cos_r, cos_r, pad_r], axis=1)
    sin_kr = jnp.concatenate([pad_l, -sin_r, sin_r, pad_r], axis=1)
    return dict(cos_ht=feature_major(cos_h), sin_ht=feature_major(sin_h),
                cos_rt=feature_major(cos_r), sin_rt=feature_major(sin_r),
                cos_k=token_major(cos_k), sin_k=token_major(sin_k),
                cos_kr=token_major(cos_kr), sin_kr=token_major(sin_kr))


def _col(v):
    return jnp.broadcast_to(v.astype(F32)[:, None], (v.shape[0], LANES))


def _ffn_weights(w_up, conv_w, conv_b, w_down):
    d = w_up.shape[0]
    n = D_FF // FF_CHUNK

    def pair(m):
        a = m[..., :D_FF].reshape(m.shape[:-1] + (n, FF_CHUNK))
        g = m[..., D_FF:].reshape(m.shape[:-1] + (n, FF_CHUNK))
        return jnp.moveaxis(jnp.concatenate([a, g], axis=-1), -2, 0)

    return (pair(w_up).astype(BF16), pair(conv_w), pair(conv_b[None, :]),
            w_down.reshape(n, FF_CHUNK, d).astype(BF16))


def kernel(x, c, ctx, c_ctx, mod_w, mod_b, norm1_g, norm2_g, ev_w_in, ev_qa_g, ev_ka_g, ev_qlat_g, ev_w_q_up, ev_kvlat_g, ev_w_kv_up, ev_w_out, od_w_in, od_ln_g, od_ln_b, od_sgu_w, od_sgu_b, od_sink, od_w_out, ffn_up, ffn_conv_w, ffn_conv_b, ffn_down, final_g):
    bsz, seq, d = x.shape
    ctx_len = ctx.shape[1]
    assert d == D_MODEL and ctx_len == TILE and seq % TILE == 0 and mod_w.shape[0] == 2
    n_lat_tiles = seq // TILE
    n_tiles = n_lat_tiles + 1

    n_rows = -(-(bsz + 1) // SUBLANES) * SUBLANES
    cc = jnp.zeros((n_rows, d), F32).at[:bsz].set(c).at[n_rows - 1].set(c_ctx)
    mods = _modulation(cc, mod_w, mod_b).reshape(2, n_rows, 6, d)
    tabs = _rope_tables(seq, ctx_len)

    w_in = ev_w_in[0]
    s0, s1, s2, s3, s4 = A_Q, A_Q + A_KV, A_Q + 2 * A_KV, A_Q + 2 * A_KV + B_Q_RANK, A_Q + 2 * A_KV + B_Q_RANK + B_KV_RANK
    w_kr = jnp.zeros((d, LANES), F32).at[:, B_NOPE:B_QK].set(w_in[:, s4:])
    w_row = jnp.concatenate([w_in[:, s0:s1], w_in[:, s3:s4], w_kr], axis=1).astype(BF16)
    w_t = jnp.concatenate([w_in[:, :s0], w_in[:, s1:s2], w_in[:, s2:s3], w_in[:, s3:s4]], axis=1).T.astype(BF16)
    wq = ev_w_q_up[0].reshape(B_Q_RANK, N_HEADS_HALF, B_QK)
    wq = jnp.pad(wq, ((0, 0), (0, 0), (0, LANES - B_QK))).reshape(B_Q_RANK, N_HEADS_HALF * LANES).T.astype(BF16)
    wkv = ev_w_kv_up[0].reshape(B_KV_RANK, N_HEADS_HALF, B_NOPE + B_V)
    wk = jnp.pad(wkv[:, :, :B_NOPE], ((0, 0), (0, 0), (0, LANES - B_NOPE)))
    wk = wk.reshape(B_KV_RANK, N_HEADS_HALF * LANES).astype(BF16)
    wv = wkv[:, :, B_NOPE:].reshape(B_KV_RANK, N_HEADS_HALF * B_V).T.astype(BF16)
    even_wts = [w_row, w_t, wq, wk, wv,
                _col(ev_qa_g[0] * (HEAD_DIM ** -0.5)), jnp.tile(ev_ka_g[0], KV_HEADS)[None, :],
                _col(ev_qlat_g[0]), _col(ev_kvlat_g[0]), ev_kvlat_g[0][None, :]]
    even_tabs = [tabs[k] for k in ("cos_ht", "sin_ht", "cos_rt", "sin_rt", "cos_k", "sin_k", "cos_kr", "sin_kr")]
    qa, ka, va, qb, kb, vb = _even_proj(x, ctx, mods[0], norm1_g[0][None, :], even_wts, even_tabs)
    y = _even_attn(qa, ka, va, qb, kb, vb)
    h = _out_proj((x, ctx), mods[0], (y, 0), (y, 0), ev_w_out[0].astype(BF16), 0, n_tiles)
    h = _conv_ffn(h, mods[0], norm2_g[0][None, :],
                  *_ffn_weights(ffn_up[0], ffn_conv_w[0], ffn_conv_b[0], ffn_down[0]), has_ctx_tile=True)

    w_in = od_w_in[0]
    o0, o1, o2, o3 = C_WIDTH, 2 * C_WIDTH, 2 * C_WIDTH + A_Q, 2 * C_WIDTH + A_Q + A_KV
    w_row = w_in[:, o2:o3].astype(BF16)
    w_t = jnp.concatenate([w_in[:, :o2], w_in[:, o3:]], axis=1).T.astype(BF16)
    ws_t = od_sgu_w[0].transpose(0, 2, 1).astype(BF16)
    odd_wts = [w_row, w_t, _col(od_ln_g[0]), _col(od_ln_b[0]), ws_t, od_sgu_b[0][:, None, :]]
    odd_tabs = [tabs[k] for k in ("cos_ht", "sin_ht", "cos_k", "sin_k")]
    yc, qd, kd, vd = _odd_proj(h, mods[1], norm1_g[1][None, :], odd_wts, odd_tabs)
    yd = _window_attn(od_sink[0], qd, kd, vd, n_lat_tiles)
    h = _out_proj(h, mods[1], (yc, 1), (yd, 0), od_w_out[0].astype(BF16), 1, n_lat_tiles)
    return _conv_ffn(h, mods[1], norm2_g[1][None, :],
                     *_ffn_weights(ffn_up[1], ffn_conv_w[1], ffn_conv_b[1], ffn_down[1]),
                     has_ctx_tile=False, final_g=final_g[None, :])
```

```python
import functools

import jax
import jax.numpy as jnp
from jax import lax
from jax.experimental import pallas as pl
from jax.experimental.pallas import tpu as pltpu

F32 = jnp.float32
BF16 = jnp.bfloat16

D_MODEL = 1024
HEAD_DIM = 64
GRID_W = 64
ROPE_THETA = 10000.0
NORM_EPS = 1e-6
NEG_INF = -1e30
LOG2_E = 1.4426950408889634
N_HEADS_HALF = 8
KV_HEADS = 2
GQA_GROUP = N_HEADS_HALF // KV_HEADS
A_Q = N_HEADS_HALF * HEAD_DIM
A_KV = KV_HEADS * HEAD_DIM
B_Q_RANK = 256
B_KV_RANK = 128
B_NOPE = 64
B_ROPE = 32
B_V = 64
B_QK = B_NOPE + B_ROPE
C_WIDTH = 512
C_GROUPS = 8
C_CHUNK = 128
WINDOW = 128
D_FF = 2816
LANES = 128
SUBLANES = 8
TILE = 256
FF_CHUNK = 256
VMEM_LIMIT = 56 * 1024 * 1024


def _dot(a, b):
    return jnp.dot(a, b, preferred_element_type=F32)


def _dot_nt(a, b):
    return lax.dot_general(a, b, (((1,), (1,)), ((), ())), preferred_element_type=F32)


def _dot_tn(a, b):
    return lax.dot_general(a, b, (((0,), (0,)), ((), ())), preferred_element_type=F32)


def _lane_tile(v, width):
    return jnp.concatenate([v] * (width // LANES), axis=1)


def _norm_modulate(h, g, shift, scale):
    ms = jnp.mean(h * h, axis=-1, keepdims=True)
    y = h * lax.rsqrt(ms + NORM_EPS) * g
    return y * (1.0 + scale) + shift


def _rope_rows(x, cos, sin):
    half = cos.shape[0]
    x1 = x[..., :half, :]
    x2 = x[..., half:, :]
    return jnp.concatenate([x1 * cos - x2 * sin, x1 * sin + x2 * cos], axis=-2)


def _rope_lanes(x, cos, sin_signed, half, first_half_mask):
    up = pltpu.roll(x, LANES - half, 1)
    down = pltpu.roll(x, half, 1)
    return x * cos + jnp.where(first_half_mask, up, down) * sin_signed


def _mod_kernel(c_ref, w_ref, b_ref, o_ref):
    c = c_ref[...]
    a = c * jax.nn.sigmoid(c)
    o_ref[...] = jnp.dot(a, w_ref[...], preferred_element_type=F32,
                         precision=lax.Precision.HIGHEST) + b_ref[...]


def _modulation(cc, mod_w, mod_b):
    depth, d, n = mod_w.shape
    rows = cc.shape[0]
    bn = 1024
    return pl.pallas_call(
        _mod_kernel,
        grid=(depth, n // bn),
        in_specs=[
            pl.BlockSpec((rows, d), lambda l, j: (0, 0)),
            pl.BlockSpec((None, d, bn), lambda l, j: (l, 0, j)),
            pl.BlockSpec((None, 1, bn), lambda l, j: (l, 0, j)),
        ],
        out_specs=pl.BlockSpec((None, rows, bn), lambda l, j: (l, 0, j)),
        out_shape=jax.ShapeDtypeStruct((depth, rows, n), F32),
        compiler_params=pltpu.CompilerParams(dimension_semantics=("arbitrary", "arbitrary")),
        name="modulation",
    )(cc, mod_w, mod_b.reshape(depth, 1, n))


def _even_proj_kernel(x_ref, ctx_ref, mod_ref, g1_ref, w_row_ref, w_t_ref, wq_up_ref, wk_up_ref, wv_up_ref,
                      qa_g_ref, ka_g_ref, qlat_g_ref, kvlat_gc_ref, kvlat_gr_ref,
                      cos_ht_ref, sin_ht_ref, cos_rt_ref, sin_rt_ref,
                      cos_k_ref, sin_k_ref, cos_kr_ref, sin_kr_ref,
                      qa_ref, ka_ref, va_ref, qb_ref, kb_ref, vb_ref):
    t = pl.program_id(1)
    tile = x_ref.shape[0]
    h = jnp.where(t == 0, ctx_ref[...], x_ref[...])
    z = _norm_modulate(h, g1_ref[...], mod_ref[0:1, :], mod_ref[1:2, :]).astype(BF16)
    hr = _dot(z, w_row_ref[...])
    ht = _dot_nt(w_t_ref[...], z)
    lane = lax.broadcasted_iota(jnp.int32, (tile, LANES), 1)

    qa = ht[0:A_Q].reshape(N_HEADS_HALF, HEAD_DIM, tile)
    r = lax.rsqrt(jnp.mean(qa * qa, axis=1, keepdims=True) + NORM_EPS)
    qa = qa * r * _lane_tile(qa_g_ref[...], tile)
    qa = _rope_rows(qa, cos_ht_ref[...], sin_ht_ref[...])
    qa_ref[...] = qa.reshape(A_Q, tile).astype(BF16)

    ka = hr[:, 0:A_KV]
    sq = ka * ka
    head0 = lane < HEAD_DIM
    ss0 = jnp.sum(jnp.where(head0, sq, 0.0), axis=-1, keepdims=True)
    ss1 = jnp.sum(jnp.where(head0, 0.0, sq), axis=-1, keepdims=True)
    r = lax.rsqrt(jnp.where(head0, ss0, ss1) * (1.0 / HEAD_DIM) + NORM_EPS)
    ka = ka * r * ka_g_ref[...]
    ka = _rope_lanes(ka, cos_k_ref[...], sin_k_ref[...], HEAD_DIM // 2,
                     (lane & (HEAD_DIM - 1)) < HEAD_DIM // 2)
    ka_ref[...] = ka.astype(BF16)

    va_ref[...] = ht[A_Q:A_Q + A_KV].astype(BF16)

    o = A_Q + A_KV
    cq = ht[o:o + B_Q_RANK]
    r = lax.rsqrt(jnp.mean(cq * cq, axis=0, keepdims=True) + NORM_EPS)
    cq = (cq * r * _lane_tile(qlat_g_ref[...], tile)).astype(BF16)
    qb = _dot(wq_up_ref[...], cq).reshape(N_HEADS_HALF, LANES, tile)
    q_rope = _rope_rows(qb[:, B_NOPE:B_QK, :], cos_rt_ref[...], sin_rt_ref[...])
    qb = jnp.concatenate([qb[:, :B_NOPE, :], q_rope, qb[:, B_QK:, :]], axis=1) * (B_QK ** -0.5 * LOG2_E)
    qb_ref[...] = qb.reshape(N_HEADS_HALF * LANES, tile).astype(BF16)

    ckv = hr[:, A_KV:A_KV + B_KV_RANK]
    r = lax.rsqrt(jnp.mean(ckv * ckv, axis=-1, keepdims=True) + NORM_EPS)
    ckv = (ckv * r * kvlat_gr_ref[...]).astype(BF16)
    k_nope = _dot(ckv, wk_up_ref[...])
    kr = hr[:, A_KV + B_KV_RANK:A_KV + B_KV_RANK + LANES]
    kr = _rope_lanes(kr, cos_kr_ref[...], sin_kr_ref[...], B_ROPE // 2, lane < B_NOPE + B_ROPE // 2)
    for hd in range(N_HEADS_HALF):
        kb_ref[hd] = (k_nope[:, hd * LANES:(hd + 1) * LANES] + kr).astype(BF16)

    o = A_Q + A_KV + B_Q_RANK
    ckvt = ht[o:o + B_KV_RANK]
    r = lax.rsqrt(jnp.mean(ckvt * ckvt, axis=0, keepdims=True) + NORM_EPS)
    ckvt = (ckvt * r * _lane_tile(kvlat_gc_ref[...], tile)).astype(BF16)
    vb_ref[...] = _dot(wv_up_ref[...], ckvt).astype(BF16)


def _const_spec(shape):
    nd = len(shape)
    return pl.BlockSpec(shape, lambda b, t, _n=nd: (0,) * _n)


def _tile_spec(rows, cols):
    return pl.BlockSpec((None, None, rows, cols), lambda b, t: (b, t, 0, 0))


def _even_proj(x, ctx, mod, g1, wts, tabs):
    bsz, seq, d = x.shape
    n_tiles = seq // TILE + 1
    n_rows = mod.shape[0]

    def mod_map(b, t):
        return (jnp.where(t == 0, n_rows - 1, b), 0, 0)

    in_specs = [
        pl.BlockSpec((None, TILE, d), lambda b, t: (b, jnp.maximum(t - 1, 0), 0)),
        pl.BlockSpec((None, TILE, d), lambda b, t: (b, 0, 0)),
        pl.BlockSpec((None, 6, d), mod_map),
        _const_spec(g1.shape),
    ]
    in_specs += [_const_spec(w.shape) for w in wts]
    in_specs += [pl.BlockSpec((None,) + tb.shape[1:], lambda b, t: (t, 0, 0)) for tb in tabs]
    t_total = n_tiles * TILE

    def seq_spec(rows):
        return pl.BlockSpec((None, rows, TILE), lambda b, t: (b, 0, t))

    out_specs = [
        _tile_spec(A_Q, TILE), _tile_spec(TILE, A_KV), seq_spec(A_KV),
        _tile_spec(N_HEADS_HALF * LANES, TILE),
        pl.BlockSpec((None, None, N_HEADS_HALF, TILE, LANES), lambda b, t: (b, t, 0, 0, 0)),
        seq_spec(N_HEADS_HALF * B_V),
    ]
    out_shapes = [
        (bsz, n_tiles, A_Q, TILE), (bsz, n_tiles, TILE, A_KV), (bsz, A_KV, t_total),
        (bsz, n_tiles, N_HEADS_HALF * LANES, TILE), (bsz, n_tiles, N_HEADS_HALF, TILE, LANES),
        (bsz, N_HEADS_HALF * B_V, t_total),
    ]
    return pl.pallas_call(
        _even_proj_kernel,
        grid=(bsz, n_tiles),
        in_specs=in_specs,
        out_specs=out_specs,
        out_shape=[jax.ShapeDtypeStruct(shp, BF16) for shp in out_shapes],
        compiler_params=pltpu.CompilerParams(dimension_semantics=("parallel", "arbitrary"),
                                             vmem_limit_bytes=VMEM_LIMIT),
        name="even_proj",
    )(x, ctx, mod, g1, *wts, *tabs)


def _scores(keys, q, s_ref):
    n_keys = keys.shape[0]
    tile = q.shape[1]
    s = _dot(keys, q)
    s_ref[0:n_keys, :] = s
    m = jnp.max(s.reshape(n_keys // SUBLANES, SUBLANES, tile), axis=0)
    return jnp.max(m, axis=0, keepdims=True)


def _weighted_values(s_ref, n_keys, m, value_chunk, dv):
    tile = s_ref.shape[1]
    groups = TILE // SUBLANES
    l = jnp.zeros((SUBLANES, tile), F32)
    acc = jnp.zeros((dv, tile), F32)
    for c in range(n_keys // TILE):
        p = jnp.exp2(s_ref[c * TILE:(c + 1) * TILE, :] - m)
        l = l + jnp.sum(p.reshape(groups, SUBLANES, tile), axis=0)
        acc = acc + _dot(value_chunk(c), p.astype(BF16))
    l = jnp.sum(l, axis=0, keepdims=True)
    return acc * (1.0 / l)


def _even_attn_kernel(qa_ref, qb_ref, ka_ref, va_ref, kb_ref, vb_ref, y_ref, s0_ref, s1_ref):
    t = pl.program_id(1)
    n_tiles = ka_ref.shape[0]
    s_refs = (s0_ref, s1_ref)

    def rows_of(index, size):
        return pl.ds(pl.multiple_of(index * size, size), size)

    def run(n_key_tiles):
        n_keys = n_key_tiles * TILE

        def a_scores(hd, s_ref):
            q = qa_ref[rows_of(hd, HEAD_DIM), :]
            zeros = jnp.zeros_like(q)
            first_kv = hd < GQA_GROUP
            q = jnp.concatenate([jnp.where(first_kv, q, zeros), jnp.where(first_kv, zeros, q)], axis=0)
            return _scores(ka_ref[0:n_key_tiles].reshape(n_keys, A_KV), q, s_ref)

        def a_output(hd, s_ref, m):
            v_rows = rows_of(hd // GQA_GROUP, HEAD_DIM)
            out = _weighted_values(s_ref, n_keys, m,
                                   lambda c: va_ref[v_rows, c * TILE:(c + 1) * TILE], HEAD_DIM)
            y_ref[rows_of(hd, HEAD_DIM), :] = out.astype(BF16)

        def b_scores(hd, s_ref):
            return _scores(kb_ref[0:n_key_tiles, hd].reshape(n_keys, LANES),
                           qb_ref[rows_of(hd, LANES), :], s_ref)

        def b_output(hd, s_ref, m):
            v_rows = rows_of(hd, B_V)
            out = _weighted_values(s_ref, n_keys, m,
                                   lambda c: vb_ref[v_rows, c * TILE:(c + 1) * TILE], B_V)
            y_ref[rows_of(N_HEADS_HALF + hd, B_V), :] = out.astype(BF16)

        def pair_step(scores, output):
            def step(i, m):
                hd = 2 * i
                m_next = scores(hd + 1, s_refs[1])
                output(hd, s_refs[0], m)
                m = scores(hd + 2, s_refs[0])
                output(hd + 1, s_refs[1], m_next)
                return m
            return step

        last = N_HEADS_HALF - 1
        m = lax.fori_loop(0, last // 2, pair_step(a_scores, a_output), a_scores(0, s_refs[0]))
        m_next = a_scores(last, s_refs[1])
        a_output(last - 1, s_refs[0], m)
        m = b_scores(0, s_refs[0])
        a_output(last, s_refs[1], m_next)
        m = lax.fori_loop(0, last // 2, pair_step(b_scores, b_output), m)
        m_next = b_scores(last, s_refs[1])
        b_output(last - 1, s_refs[0], m)
        b_output(last, s_refs[1], m_next)

    @pl.when(t == 0)
    def _():
        run(1)

    @pl.when(t != 0)
    def _():
        run(n_tiles)


def _even_attn(qa, ka, va, qb, kb, vb):
    bsz, n_tiles = qa.shape[:2]

    def batch_spec(arr):
        nd = arr.ndim - 1
        return pl.BlockSpec((None,) + arr.shape[1:], lambda b, t: (b,) + (0,) * nd)

    return pl.pallas_call(
        _even_attn_kernel,
        grid=(bsz, n_tiles),
        in_specs=[_tile_spec(*qa.shape[2:]), _tile_spec(*qb.shape[2:]),
                  batch_spec(ka), batch_spec(va), batch_spec(kb), batch_spec(vb)],
        out_specs=_tile_spec(D_MODEL, TILE),
        out_shape=jax.ShapeDtypeStruct((bsz, n_tiles, D_MODEL, TILE), BF16),
        scratch_shapes=[pltpu.VMEM((n_tiles * TILE, TILE), F32)] * 2,
        compiler_params=pltpu.CompilerParams(dimension_semantics=("parallel", "arbitrary"),
                                             vmem_limit_bytes=VMEM_LIMIT),
        name="even_attn",
    )(qa, qb, ka, va, kb, vb)


def _out_proj_kernel(two_inputs, *refs):
    if two_inputs:
        x_ref, ctx_ref, mod_ref, y0_ref, y1_ref, w0_ref, w1_ref, o_ref = refs
        h = jnp.where(pl.program_id(1) == 0, ctx_ref[...], x_ref[...])
    else:
        x_ref, mod_ref, y0_ref, y1_ref, w0_ref, w1_ref, o_ref = refs
        h = x_ref[...]
    y = _dot_tn(y0_ref[...], w0_ref[...]) + _dot_tn(y1_ref[...], w1_ref[...])
    o_ref[...] = h + mod_ref[2:3, :] * y


def _out_proj(h_in, mod, y0, y1, w_out, tile_offset, n_out_tiles):
    two_inputs = isinstance(h_in, tuple)
    (y0, off0), (y1, off1) = y0, y1
    bsz = y0.shape[0]
    d = w_out.shape[1]
    n_rows = mod.shape[0]
    half = w_out.shape[0] // 2

    def mod_map(b, t):
        return (jnp.where(t + tile_offset == 0, n_rows - 1, b), 0, 0)

    if two_inputs:
        x, ctx = h_in
        h_specs = [pl.BlockSpec((None, TILE, d), lambda b, t: (b, jnp.maximum(t - 1, 0), 0)),
                   pl.BlockSpec((None, TILE, d), lambda b, t: (b, 0, 0))]
        h_args = [x, ctx]
    else:
        h_specs = [pl.BlockSpec((None, TILE, d), lambda b, t: (b, t + tile_offset, 0))]
        h_args = [h_in]

    def y_spec(arr, off, part):
        return pl.BlockSpec((None, None, half, TILE), lambda b, t: (b, t + off, part, 0))

    same = y0 is y1
    return pl.pallas_call(
        functools.partial(_out_proj_kernel, two_inputs),
        grid=(bsz, n_out_tiles),
        in_specs=h_specs + [
            pl.BlockSpec((None, 6, d), mod_map),
            y_spec(y0, off0, 0), y_spec(y1, off1, 1 if same else 0),
            pl.BlockSpec((half, d), lambda b, t: (0, 0)),
            pl.BlockSpec((half, d), lambda b, t: (1, 0)),
        ],
        out_specs=pl.BlockSpec((None, TILE, d), lambda b, t: (b, t, 0)),
        out_shape=jax.ShapeDtypeStruct((bsz, n_out_tiles * TILE, d), F32),
        compiler_params=pltpu.CompilerParams(dimension_semantics=("parallel", "arbitrary"),
                                             vmem_limit_bytes=VMEM_LIMIT),
        name="out_proj",
    )(*h_args, mod, y0, y1, w_out, w_out)


def _conv_ffn_kernel(has_ctx_tile, final_norm, *refs):
    if final_norm:
        (h_ref, prev_ref, next_ref, mod_ref, g2_ref, w_up_ref, cw_ref, cb_ref, w_down_ref, gf_ref,
         o_ref, hs_ref, acc_ref) = refs
    else:
        (h_ref, prev_ref, next_ref, mod_ref, g2_ref, w_up_ref, cw_ref, cb_ref, w_down_ref,
         o_ref, hs_ref, acc_ref) = refs
    t = pl.program_id(1)
    last_tile = pl.num_programs(1) - 1
    tile = h_ref.shape[0]
    n_chunks = w_up_ref.shape[0]
    g2 = g2_ref[...]
    shift, scale = mod_ref[3:4, :], mod_ref[4:5, :]
    if has_ctx_tile:
        has_prev = jnp.logical_and(t != 0, t != 1).astype(F32)
        has_next = jnp.logical_and(t != 0, t != last_tile).astype(F32)
    else:
        has_prev = (t != 0).astype(F32)
        has_next = (t != last_tile).astype(F32)
    h = h_ref[...]
    z = jnp.concatenate([
        _norm_modulate(prev_ref[...], g2, shift, scale) * has_prev,
        _norm_modulate(h, g2, shift, scale),
        _norm_modulate(next_ref[...], g2, shift, scale) * has_next,
    ], axis=0).astype(BF16)

    acc_ref[...] = jnp.zeros_like(acc_ref)

    def chunk(j, carry):
        hs_ref[...] = _dot(z, w_up_ref[j])
        cw = cw_ref[j]
        conv = (hs_ref[SUBLANES - 1:SUBLANES - 1 + tile, :] * cw[0:1, :]
                + hs_ref[SUBLANES:SUBLANES + tile, :] * cw[1:2, :]
                + hs_ref[SUBLANES + 1:SUBLANES + 1 + tile, :] * cw[2:3, :]
                + cb_ref[j])
        a = conv[:, :FF_CHUNK]
        g = conv[:, FF_CHUNK:]
        act = (g * jax.nn.sigmoid(g) * a).astype(BF16)
        acc_ref[...] += _dot(act, w_down_ref[j])
        return carry

    lax.fori_loop(0, n_chunks, chunk, 0)
    out = h + mod_ref[5:6, :] * acc_ref[...]
    if final_norm:
        ms = jnp.mean(out * out, axis=-1, keepdims=True)
        out = out * lax.rsqrt(ms + NORM_EPS) * gf_ref[...]
    o_ref[...] = out


def _conv_ffn(h, mod, g2, w_up, conv_w, conv_b, w_down, has_ctx_tile, final_g=None):
    bsz, t_total, d = h.shape
    n_rows = mod.shape[0]
    n_out_tiles = t_total // TILE
    rows_per_tile = TILE // SUBLANES
    last_block = t_total // SUBLANES - 1
    final_norm = final_g is not None

    def mod_map(b, t):
        return (jnp.where(t == 0, n_rows - 1, b) if has_ctx_tile else b, 0, 0)

    in_specs = [
        pl.BlockSpec((None, TILE, d), lambda b, t: (b, t, 0)),
        pl.BlockSpec((None, SUBLANES, d), lambda b, t: (b, jnp.maximum(t * rows_per_tile - 1, 0), 0)),
        pl.BlockSpec((None, SUBLANES, d),
                     lambda b, t: (b, jnp.minimum((t + 1) * rows_per_tile, last_block), 0)),
        pl.BlockSpec((None, 6, d), mod_map),
        _const_spec(g2.shape), _const_spec(w_up.shape), _const_spec(conv_w.shape),
        _const_spec(conv_b.shape), _const_spec(w_down.shape),
    ]
    args = [h, h, h, mod, g2, w_up, conv_w, conv_b, w_down]
    if final_norm:
        in_specs.append(_const_spec(final_g.shape))
        args.append(final_g)
    return pl.pallas_call(
        functools.partial(_conv_ffn_kernel, has_ctx_tile, final_norm),
        grid=(bsz, n_out_tiles),
        in_specs=in_specs,
        out_specs=pl.BlockSpec((None, TILE, d), lambda b, t: (b, t, 0)),
        out_shape=jax.ShapeDtypeStruct((bsz, n_out_tiles * TILE, d), F32),
        scratch_shapes=[pltpu.VMEM((TILE + 2 * SUBLANES, 2 * FF_CHUNK), F32),
                        pltpu.VMEM((TILE, d), F32)],
        compiler_params=pltpu.CompilerParams(dimension_semantics=("parallel", "arbitrary"),
                                             vmem_limit_bytes=VMEM_LIMIT),
        name="conv_ffn",
    )(*args)


def _gelu(x):
    return 0.5 * x * (1.0 + jnp.tanh(0.7978845608028654 * (x + 0.044715 * (x * x * x))))


def _odd_proj_kernel(h_ref, mod_ref, g1_ref, w_row_ref, w_t_ref, ln_g_ref, ln_b_ref, ws_ref, bs_ref,
                     cos_ht_ref, sin_ht_ref, cos_k_ref, sin_k_ref,
                     yc_ref, qd_ref, kd_ref, vd_ref):
    tile = h_ref.shape[0]
    z = _norm_modulate(h_ref[...], g1_ref[...], mod_ref[0:1, :], mod_ref[1:2, :]).astype(BF16)
    hr = _dot(z, w_row_ref[...])
    ht = _dot_nt(w_t_ref[...], z)
    lane = lax.broadcasted_iota(jnp.int32, (tile, LANES), 1)

    v = _gelu(ht[C_WIDTH:2 * C_WIDTH])
    mu = jnp.mean(v, axis=0, keepdims=True)
    vc = v - mu
    var = jnp.mean(vc * vc, axis=0, keepdims=True)
    vn = vc * lax.rsqrt(var + NORM_EPS) * _lane_tile(ln_g_ref[...], tile) + _lane_tile(ln_b_ref[...], tile)
    vn = vn.astype(BF16)
    gw = C_WIDTH // C_GROUPS
    for g in range(C_GROUPS):
        for n in range(tile // C_CHUNK):
            cols = slice(n * C_CHUNK, (n + 1) * C_CHUNK)
            rows = slice(g * gw, (g + 1) * gw)
            mixed = _dot(vn[rows, cols], ws_ref[g]) + bs_ref[g]
            yc_ref[rows, cols] = (_gelu(ht[rows, cols]) * mixed).astype(BF16)

    o = 2 * C_WIDTH
    qd = ht[o:o + A_Q].reshape(N_HEADS_HALF, HEAD_DIM, tile)
    qd = _rope_rows(qd, cos_ht_ref[...], sin_ht_ref[...]) * (HEAD_DIM ** -0.5)
    qd_ref[...] = qd.reshape(A_Q, tile).astype(BF16)
    kd = _rope_lanes(hr, cos_k_ref[...], sin_k_ref[...], HEAD_DIM // 2,
                     (lane & (HEAD_DIM - 1)) < HEAD_DIM // 2).astype(BF16)
    vd = ht[o + A_Q:o + A_Q + A_KV].astype(BF16)
    for n in range(tile // C_CHUNK):
        kd_ref[n] = kd[n * C_CHUNK:(n + 1) * C_CHUNK, :]
        vd_ref[n] = vd[:, n * C_CHUNK:(n + 1) * C_CHUNK]


def _odd_proj(h, mod, g1, wts, tabs):
    bsz, t_total, d = h.shape
    n_tiles = t_total // TILE
    n_rows = mod.shape[0]
    sub = TILE // C_CHUNK

    def mod_map(b, t):
        return (jnp.where(t == 0, n_rows - 1, b), 0, 0)

    in_specs = [
        pl.BlockSpec((None, TILE, d), lambda b, t: (b, t, 0)),
        pl.BlockSpec((None, 6, d), mod_map),
        _const_spec(g1.shape),
    ]
    in_specs += [_const_spec(w.shape) for w in wts]
    in_specs += [pl.BlockSpec((None,) + tb.shape[1:], lambda b, t: (t, 0, 0)) for tb in tabs]
    chunk_spec = pl.BlockSpec((None, sub, C_CHUNK, C_CHUNK), lambda b, t: (b, t, 0, 0))
    return pl.pallas_call(
        _odd_proj_kernel,
        grid=(bsz, n_tiles),
        in_specs=in_specs,
        out_specs=[_tile_spec(C_WIDTH, TILE), _tile_spec(A_Q, TILE), chunk_spec, chunk_spec],
        out_shape=[jax.ShapeDtypeStruct((bsz, n_tiles, C_WIDTH, TILE), BF16),
                   jax.ShapeDtypeStruct((bsz, n_tiles, A_Q, TILE), BF16),
                   jax.ShapeDtypeStruct((bsz, n_tiles * sub, C_CHUNK, A_KV), BF16),
                   jax.ShapeDtypeStruct((bsz, n_tiles * sub, A_KV, C_CHUNK), BF16)],
        compiler_params=pltpu.CompilerParams(dimension_semantics=("parallel", "arbitrary"),
                                             vmem_limit_bytes=VMEM_LIMIT),
        name="odd_proj",
    )(h, mod, g1, *wts, *tabs)


def _window_attn_kernel(sink_ref, qd_ref, kd_ref, vd_ref, y_ref):
    t = pl.program_id(1)
    tile = qd_ref.shape[1]
    n_kchunks = kd_ref.shape[0]
    ctx_chunks = TILE // C_CHUNK
    sub = tile // C_CHUNK
    win_chunks = sub + 2
    lat_chunks = n_kchunks - ctx_chunks
    w0 = jnp.clip(t * sub - 1, 0, lat_chunks - win_chunks)
    n_keys = (ctx_chunks + win_chunks) * C_CHUNK

    k_ctx = kd_ref[0:ctx_chunks].reshape(ctx_chunks * C_CHUNK, A_KV)
    k_win = kd_ref[pl.ds(ctx_chunks + w0, win_chunks)].reshape(win_chunks * C_CHUNK, A_KV)
    keys = jnp.concatenate([k_ctx, k_win], axis=0)
    key_pos = w0 * C_CHUNK + lax.broadcasted_iota(jnp.int32, (win_chunks * C_CHUNK, tile), 0)
    qry_pos = t * tile + lax.broadcasted_iota(jnp.int32, (win_chunks * C_CHUNK, tile), 1)
    band = jnp.abs(qry_pos - key_pos) <= WINDOW
    zeros = jnp.zeros((HEAD_DIM, tile), BF16)
    groups = n_keys // SUBLANES
    for hd in range(N_HEADS_HALF):
        kv = hd // GQA_GROUP
        q = qd_ref[hd * HEAD_DIM:(hd + 1) * HEAD_DIM, :]
        q = jnp.concatenate([q, zeros] if kv == 0 else [zeros, q], axis=0)
        s = _dot(keys, q)
        s = jnp.concatenate([s[:ctx_chunks * C_CHUNK],
                             jnp.where(band, s[ctx_chunks * C_CHUNK:], NEG_INF)], axis=0)
        sink = sink_ref[hd]
        m = jnp.max(s.reshape(groups, SUBLANES, tile), axis=0)
        m = jnp.maximum(jnp.max(m, axis=0, keepdims=True), sink)
        p = jnp.exp(s - m)
        l = jnp.sum(p.reshape(groups, SUBLANES, tile), axis=0)
        l = jnp.sum(l, axis=0, keepdims=True) + jnp.exp(sink - m)
        p = p.astype(BF16)
        rows = slice(kv * HEAD_DIM, (kv + 1) * HEAD_DIM)
        acc = jnp.zeros((HEAD_DIM, tile), F32)
        for c in range(ctx_chunks + win_chunks):
            vchunk = vd_ref[c, rows, :] if c < ctx_chunks else vd_ref[w0 + c, rows, :]
            acc = acc + _dot(vchunk, p[c * C_CHUNK:(c + 1) * C_CHUNK, :])
        y_ref[hd * HEAD_DIM:(hd + 1) * HEAD_DIM, :] = (acc * (1.0 / l)).astype(BF16)


def _window_attn(sink, qd, kd, vd, n_lat_tiles):
    bsz = qd.shape[0]

    def batch_spec(arr):
        return pl.BlockSpec((None,) + arr.shape[1:], lambda b, t: (b, 0, 0, 0))

    return pl.pallas_call(
        _window_attn_kernel,
        grid=(bsz, n_lat_tiles),
        in_specs=[pl.BlockSpec(memory_space=pltpu.SMEM),
                  pl.BlockSpec((None, None, A_Q, TILE), lambda b, t: (b, t + 1, 0, 0)),
                  batch_spec(kd), batch_spec(vd)],
        out_specs=_tile_spec(A_Q, TILE),
        out_shape=jax.ShapeDtypeStruct((bsz, n_lat_tiles, A_Q, TILE), BF16),
        compiler_params=pltpu.CompilerParams(dimension_semantics=("parallel", "arbitrary"),
                                             vmem_limit_bytes=VMEM_LIMIT),
        name="window_attn",
    )(sink, qd, kd, vd)


def _rope_tables(seq, ctx_len):
    def base(dim):
        n_rows = seq // GRID_W
        rows = jnp.repeat(jnp.arange(n_rows), GRID_W).astype(F32)
        cols = jnp.tile(jnp.arange(GRID_W), n_rows).astype(F32)
        quarter = dim // 4
        inv_freq = ROPE_THETA ** (-jnp.arange(quarter, dtype=F32) / quarter)
        ang = jnp.concatenate([rows[:, None] * inv_freq, cols[:, None] * inv_freq], axis=-1)
        cos = jnp.concatenate([jnp.ones((ctx_len, dim // 2), F32), jnp.cos(ang)], axis=0)
        sin = jnp.concatenate([jnp.zeros((ctx_len, dim // 2), F32), jnp.sin(ang)], axis=0)
        return cos, sin

    n_tiles = (seq + ctx_len) // TILE

    def feature_major(tb):
        return tb.reshape(n_tiles, TILE, tb.shape[1]).transpose(0, 2, 1)

    def token_major(tb):
        return tb.reshape(n_tiles, TILE, LANES)

    cos_h, sin_h = base(HEAD_DIM)
    cos_r, sin_r = base(B_ROPE)
    cos_k = jnp.tile(cos_h, (1, LANES // (HEAD_DIM // 2)))
    sin_k = jnp.tile(jnp.concatenate([-sin_h, sin_h], axis=1), (1, LANES // HEAD_DIM))
    t_total = seq + ctx_len
    pad_l = jnp.zeros((t_total, B_NOPE), F32)
    pad_r = jnp.zeros((t_total, LANES - B_QK), F32)
    cos_kr = jnp.concatenate([pad_l, cos_r, cos_r, pad_r], axis=1)
    sin_kr = jnp.concatenate([pad_l, -sin_r, sin_r, pad_r], axis=1)
    return dict(cos_ht=feature_major(cos_h), sin_ht=feature_major(sin_h),
                cos_rt=feature_major(cos_r), sin_rt=feature_major(sin_r),
                cos_k=token_major(cos_k), sin_k=token_major(sin_k),
                cos_kr=token_major(cos_kr), sin_kr=token_major(sin_kr))


def _col(v):
    return jnp.broadcast_to(v.astype(F32)[:, None], (v.shape[0], LANES))


def _ffn_weights(w_up, conv_w, conv_b, w_down):
    d = w_up.shape[0]
    n = D_FF // FF_CHUNK

    def pair(m):
        a = m[..., :D_FF].reshape(m.shape[:-1] + (n, FF_CHUNK))
        g = m[..., D_FF:].reshape(m.shape[:-1] + (n, FF_CHUNK))
        return jnp.moveaxis(jnp.concatenate([a, g], axis=-1), -2, 0)

    return (pair(w_up).astype(BF16), pair(conv_w), pair(conv_b[None, :]),
            w_down.reshape(n, FF_CHUNK, d).astype(BF16))


def kernel(x, c, ctx, c_ctx, mod_w, mod_b, norm1_g, norm2_g, ev_w_in, ev_qa_g, ev_ka_g, ev_qlat_g, ev_w_q_up, ev_kvlat_g, ev_w_kv_up, ev_w_out, od_w_in, od_ln_g, od_ln_b, od_sgu_w, od_sgu_b, od_sink, od_w_out, ffn_up, ffn_conv_w, ffn_conv_b, ffn_down, final_g):
    bsz, seq, d = x.shape
    ctx_len = ctx.shape[1]
    assert d == D_MODEL and ctx_len == TILE and seq % TILE == 0 and mod_w.shape[0] == 2
    n_lat_tiles = seq // TILE
    n_tiles = n_lat_tiles + 1

    n_rows = -(-(bsz + 1) // SUBLANES) * SUBLANES
    cc = jnp.zeros((n_rows, d), F32).at[:bsz].set(c).at[n_rows - 1].set(c_ctx)
    mods = _modulation(cc, mod_w, mod_b).reshape(2, n_rows, 6, d)
    tabs = _rope_tables(seq, ctx_len)

    w_in = ev_w_in[0]
    s0, s1, s2, s3, s4 = A_Q, A_Q + A_KV, A_Q + 2 * A_KV, A_Q + 2 * A_KV + B_Q_RANK, A_Q + 2 * A_KV + B_Q_RANK + B_KV_RANK
    w_kr = jnp.zeros((d, LANES), F32).at[:, B_NOPE:B_QK].set(w_in[:, s4:])
    w_row = jnp.concatenate([w_in[:, s0:s1], w_in[:, s3:s4], w_kr], axis=1).astype(BF16)
    w_t = jnp.concatenate([w_in[:, :s0], w_in[:, s1:s2], w_in[:, s2:s3], w_in[:, s3:s4]], axis=1).T.astype(BF16)
    wq = ev_w_q_up[0].reshape(B_Q_RANK, N_HEADS_HALF, B_QK)
    wq = jnp.pad(wq, ((0, 0), (0, 0), (0, LANES - B_QK))).reshape(B_Q_RANK, N_HEADS_HALF * LANES).T.astype(BF16)
    wkv = ev_w_kv_up[0].reshape(B_KV_RANK, N_HEADS_HALF, B_NOPE + B_V)
    wk = jnp.pad(wkv[:, :, :B_NOPE], ((0, 0), (0, 0), (0, LANES - B_NOPE)))
    wk = wk.reshape(B_KV_RANK, N_HEADS_HALF * LANES).astype(BF16)
    wv = wkv[:, :, B_NOPE:].reshape(B_KV_RANK, N_HEADS_HALF * B_V).T.astype(BF16)
    even_wts = [w_row, w_t, wq, wk, wv,
                _col(ev_qa_g[0] * (HEAD_DIM ** -0.5 * LOG2_E)), jnp.tile(ev_ka_g[0], KV_HEADS)[None, :],
                _col(ev_qlat_g[0]), _col(ev_kvlat_g[0]), ev_kvlat_g[0][None, :]]
    even_tabs = [tabs[k] for k in ("cos_ht", "sin_ht", "cos_rt", "sin_rt", "cos_k", "sin_k", "cos_kr", "sin_kr")]
    qa, ka, va, qb, kb, vb = _even_proj(x, ctx, mods[0], norm1_g[0][None, :], even_wts, even_tabs)
    y = _even_attn(qa, ka, va, qb, kb, vb)
    h = _out_proj((x, ctx), mods[0], (y, 0), (y, 0), ev_w_out[0].astype(BF16), 0, n_tiles)
    h = _conv_ffn(h, mods[0], norm2_g[0][None, :],
                  *_ffn_weights(ffn_up[0], ffn_conv_w[0], ffn_conv_b[0], ffn_down[0]), has_ctx_tile=True)

    w_in = od_w_in[0]
    o0, o1, o2, o3 = C_WIDTH, 2 * C_WIDTH, 2 * C_WIDTH + A_Q, 2 * C_WIDTH + A_Q + A_KV
    w_row = w_in[:, o2:o3].astype(BF16)
    w_t = jnp.concatenate([w_in[:, :o2], w_in[:, o3:]], axis=1).T.astype(BF16)
    ws_t = od_sgu_w[0].transpose(0, 2, 1).astype(BF16)
    odd_wts = [w_row, w_t, _col(od_ln_g[0]), _col(od_ln_b[0]), ws_t, od_sgu_b[0][:, None, :]]
    odd_tabs = [tabs[k] for k in ("cos_ht", "sin_ht", "cos_k", "sin_k")]
    yc, qd, kd, vd = _odd_proj(h, mods[1], norm1_g[1][None, :], odd_wts, odd_tabs)
    yd = _window_attn(od_sink[0], qd, kd, vd, n_lat_tiles)
    h = _out_proj(h, mods[1], (yc, 1), (yd, 0), od_w_out[0].astype(BF16), 1, n_lat_tiles)
    return _conv_ffn(h, mods[1], norm2_g[1][None, :],
                     *_ffn_weights(ffn_up[1], ffn_conv_w[1], ffn_conv_b[1], ffn_down[1]),
                     has_ctx_tile=False, final_g=final_g[None, :])
```

```python
import functools

import jax
import jax.numpy as jnp
from jax import lax
from jax.experimental import pallas as pl
from jax.experimental.pallas import tpu as pltpu

F32 = jnp.float32
BF16 = jnp.bfloat16

D_MODEL = 1024
HEAD_DIM = 64
GRID_W = 64
ROPE_THETA = 10000.0
NORM_EPS = 1e-6
NEG_INF = -1e30
LOG2_E = 1.4426950408889634
N_HEADS_HALF = 8
KV_HEADS = 2
GQA_GROUP = N_HEADS_HALF // KV_HEADS
A_Q = N_HEADS_HALF * HEAD_DIM
A_KV = KV_HEADS * HEAD_DIM
B_Q_RANK = 256
B_KV_RANK = 128
B_NOPE = 64
B_ROPE = 32
B_V = 64
B_QK = B_NOPE + B_ROPE
C_WIDTH = 512
C_GROUPS = 8
C_CHUNK = 128
WINDOW = 128
D_FF = 2816
LANES = 128
SUBLANES = 8
BF16_ROWS = 16
TILE = 256
LATENT_STEP = 512
FF_CHUNK = 256
N_SCORE_BUFFERS = 3
VMEM_LIMIT = 56 * 1024 * 1024


def _dot(a, b):
    return jnp.dot(a, b, preferred_element_type=F32)


def _dot_nt(a, b):
    return lax.dot_general(a, b, (((1,), (1,)), ((), ())), preferred_element_type=F32)


def _dot_tn(a, b):
    return lax.dot_general(a, b, (((0,), (0,)), ((), ())), preferred_element_type=F32)


def _lane_tile(v, width):
    return jnp.concatenate([v] * (width // LANES), axis=1)


def _norm_modulate(h, g, shift, scale):
    ms = jnp.mean(h * h, axis=-1, keepdims=True)
    y = h * lax.rsqrt(ms + NORM_EPS) * g
    return y * (1.0 + scale) + shift


def _rope_rows(x, cos, sin):
    half = cos.shape[0]
    x1 = x[..., :half, :]
    x2 = x[..., half:, :]
    return jnp.concatenate([x1 * cos - x2 * sin, x1 * sin + x2 * cos], axis=-2)


def _rope_lanes(x, cos, sin_signed, half, first_half_mask):
    up = pltpu.roll(x, LANES - half, 1)
    down = pltpu.roll(x, half, 1)
    return x * cos + jnp.where(first_half_mask, up, down) * sin_signed


def _const_spec(shape):
    nd = len(shape)
    return pl.BlockSpec(shape, lambda b, t, _n=nd: (0,) * _n)


def _batch_spec(arr):
    nd = arr.ndim - 1
    return pl.BlockSpec((None,) + arr.shape[1:], lambda b, t, _n=nd: (b,) + (0,) * _n)


def _mod_spec(mod, shared_row):
    n_rows, _, d = mod.shape
    if shared_row:
        return pl.BlockSpec((None, 6, d), lambda b, t: (n_rows - 1, 0, 0))
    return pl.BlockSpec((None, 6, d), lambda b, t: (b, 0, 0))


def _params():
    return pltpu.CompilerParams(dimension_semantics=("parallel", "arbitrary"), vmem_limit_bytes=VMEM_LIMIT)


def _mod_kernel(c_ref, w_ref, b_ref, o_ref):
    c = c_ref[...]
    a = c * jax.nn.sigmoid(c)
    o_ref[...] = jnp.dot(a, w_ref[...], preferred_element_type=F32,
                         precision=lax.Precision.HIGHEST) + b_ref[...]


def _modulation(cc, mod_w, mod_b):
    depth, d, n = mod_w.shape
    rows = cc.shape[0]
    bn = 1024
    return pl.pallas_call(
        _mod_kernel,
        grid=(depth, n // bn),
        in_specs=[
            pl.BlockSpec((rows, d), lambda l, j: (0, 0)),
            pl.BlockSpec((None, d, bn), lambda l, j: (l, 0, j)),
            pl.BlockSpec((None, 1, bn), lambda l, j: (l, 0, j)),
        ],
        out_specs=pl.BlockSpec((None, rows, bn), lambda l, j: (l, 0, j)),
        out_shape=jax.ShapeDtypeStruct((depth, rows, n), F32),
        compiler_params=pltpu.CompilerParams(dimension_semantics=("arbitrary", "arbitrary")),
        name="modulation",
    )(cc, mod_w, mod_b.reshape(depth, 1, n))


def _even_proj_kernel(rotate, *refs):
    (h_ref, mod_ref, g1_ref, w_row_ref, w_t_ref, wq_up_ref, wk_up_ref, wv_up_ref,
     qa_g_ref, ka_g_ref, qlat_g_ref, kvlat_gc_ref, kvlat_gr_ref) = refs[:13]
    if rotate:
        (cos_ht_ref, sin_ht_ref, cos_rt_ref, sin_rt_ref,
         cos_k_ref, sin_k_ref, cos_kr_ref, sin_kr_ref) = refs[13:21]
    qa_ref, ka_ref, va_ref, qb_ref, kb_ref, vb_ref = refs[-6:]
    tile = TILE
    lane = lax.broadcasted_iota(jnp.int32, (tile, LANES), 1)
    head0 = lane < HEAD_DIM
    for s in range(h_ref.shape[0] // tile):
        rows = slice(s * tile, (s + 1) * tile)
        z = _norm_modulate(h_ref[rows, :], g1_ref[...], mod_ref[0:1, :], mod_ref[1:2, :]).astype(BF16)
        hr = _dot(z, w_row_ref[...])
        ht = _dot_nt(w_t_ref[...], z)

        qa = ht[0:A_Q].reshape(N_HEADS_HALF, HEAD_DIM, tile)
        r = lax.rsqrt(jnp.mean(qa * qa, axis=1, keepdims=True) + NORM_EPS)
        qa = qa * r * _lane_tile(qa_g_ref[...], tile)
        if rotate:
            qa = _rope_rows(qa, cos_ht_ref[s], sin_ht_ref[s])
        qa_ref[s] = qa.reshape(A_Q, tile).astype(BF16)

        ka = hr[:, 0:A_KV]
        sq = ka * ka
        ss0 = jnp.sum(jnp.where(head0, sq, 0.0), axis=-1, keepdims=True)
        ss1 = jnp.sum(jnp.where(head0, 0.0, sq), axis=-1, keepdims=True)
        r = lax.rsqrt(jnp.where(head0, ss0, ss1) * (1.0 / HEAD_DIM) + NORM_EPS)
        ka = ka * r * ka_g_ref[...]
        if rotate:
            ka = _rope_lanes(ka, cos_k_ref[s], sin_k_ref[s], HEAD_DIM // 2,
                             (lane & (HEAD_DIM - 1)) < HEAD_DIM // 2)
        ka_ref[s] = ka.astype(BF16)

        va_ref[:, rows] = ht[A_Q:A_Q + A_KV].astype(BF16)

        o = A_Q + A_KV
        cq = ht[o:o + B_Q_RANK]
        r = lax.rsqrt(jnp.mean(cq * cq, axis=0, keepdims=True) + NORM_EPS)
        cq = (cq * r * _lane_tile(qlat_g_ref[...], tile)).astype(BF16)
        qb = _dot(wq_up_ref[...], cq).reshape(N_HEADS_HALF, LANES, tile)
        if rotate:
            q_rope = _rope_rows(qb[:, B_NOPE:B_QK, :], cos_rt_ref[s], sin_rt_ref[s])
            qb = jnp.concatenate([qb[:, :B_NOPE, :], q_rope, qb[:, B_QK:, :]], axis=1)
        qb = qb * (B_QK ** -0.5 * LOG2_E)
        qb_ref[s] = qb.reshape(N_HEADS_HALF * LANES, tile).astype(BF16)

        ckv = hr[:, A_KV:A_KV + B_KV_RANK]
        r = lax.rsqrt(jnp.mean(ckv * ckv, axis=-1, keepdims=True) + NORM_EPS)
        ckv = (ckv * r * kvlat_gr_ref[...]).astype(BF16)
        k_nope = _dot(ckv, wk_up_ref[...])
        kr = hr[:, A_KV + B_KV_RANK:A_KV + B_KV_RANK + LANES]
        if rotate:
            kr = _rope_lanes(kr, cos_kr_ref[s], sin_kr_ref[s], B_ROPE // 2, lane < B_NOPE + B_ROPE // 2)
        for hd in range(N_HEADS_HALF):
            kb_ref[s, hd] = (k_nope[:, hd * LANES:(hd + 1) * LANES] + kr).astype(BF16)

        o = A_Q + A_KV + B_Q_RANK
        ckvt = ht[o:o + B_KV_RANK]
        r = lax.rsqrt(jnp.mean(ckvt * ckvt, axis=0, keepdims=True) + NORM_EPS)
        ckvt = (ckvt * r * _lane_tile(kvlat_gc_ref[...], tile)).astype(BF16)
        vb_ref[:, rows] = _dot(wv_up_ref[...], ckvt).astype(BF16)


def _even_proj(h, mod, shared_mod, g1, wts, tabs, step):
    bsz, seq, d = h.shape
    n_sub = step // TILE
    n_tiles = seq // TILE
    rotate = tabs is not None

    def tiles_spec(*dims):
        nd = len(dims)
        return pl.BlockSpec((None, n_sub) + dims, lambda b, t, _n=nd: (b, t) + (0,) * _n)

    def seq_spec(rows):
        return pl.BlockSpec((None, rows, step), lambda b, t: (b, 0, t))

    in_specs = [pl.BlockSpec((None, step, d), lambda b, t: (b, t, 0)), _mod_spec(mod, shared_mod),
                _const_spec(g1.shape)]
    in_specs += [_const_spec(w.shape) for w in wts]
    if rotate:
        in_specs += [pl.BlockSpec((n_sub,) + tb.shape[1:], lambda b, t: (t, 0, 0)) for tb in tabs]
    out_specs = [
        tiles_spec(A_Q, TILE), tiles_spec(TILE, A_KV), seq_spec(A_KV),
        tiles_spec(N_HEADS_HALF * LANES, TILE), tiles_spec(N_HEADS_HALF, TILE, LANES),
        seq_spec(N_HEADS_HALF * B_V),
    ]
    out_shapes = [
        (bsz, n_tiles, A_Q, TILE), (bsz, n_tiles, TILE, A_KV), (bsz, A_KV, seq),
        (bsz, n_tiles, N_HEADS_HALF * LANES, TILE), (bsz, n_tiles, N_HEADS_HALF, TILE, LANES),
        (bsz, N_HEADS_HALF * B_V, seq),
    ]
    return pl.pallas_call(
        functools.partial(_even_proj_kernel, rotate),
        grid=(bsz, seq // step),
        in_specs=in_specs,
        out_specs=out_specs,
        out_shape=[jax.ShapeDtypeStruct(shp, BF16) for shp in out_shapes],
        compiler_params=_params(),
        name="even_proj",
    )(h, mod, g1, *wts, *(tabs if rotate else ()))


def _scores(key_parts, q, s_ref):
    tile = q.shape[1]
    off = 0
    m = None
    for keys in key_parts:
        n = keys.shape[0]
        s = _dot(keys, q)
        s_ref[off:off + n, :] = s
        part = jnp.max(s.reshape(n // SUBLANES, SUBLANES, tile), axis=0)
        m = part if m is None else jnp.maximum(m, part)
        off += n
    return jnp.max(m, axis=0, keepdims=True)


def _weighted_values(s_ref, n_keys, m, value_chunk, dv):
    tile = s_ref.shape[1]
    groups = TILE // SUBLANES
    l = jnp.zeros((SUBLANES, tile), F32)
    acc = jnp.zeros((dv, tile), F32)
    for c in range(n_keys // TILE):
        p = jnp.exp2(s_ref[c * TILE:(c + 1) * TILE, :] - m)
        l = l + jnp.sum(p.reshape(groups, SUBLANES, tile), axis=0)
        acc = acc + _dot(value_chunk(c), p.astype(BF16))
    l = jnp.sum(l, axis=0, keepdims=True)
    return acc * (1.0 / l)


def _even_attn_kernel(with_latent_keys, *refs):
    h_ref, mod_ref, qa_ref, qb_ref, kac_ref, vac_ref, kbc_ref, vbc_ref = refs[:8]
    if with_latent_keys:
        kal_ref, val_ref, kbl_ref, vbl_ref = refs[8:12]
    w_out_ref, o_ref = refs[-3 - N_SCORE_BUFFERS:-1 - N_SCORE_BUFFERS]
    s_refs = refs[-1 - N_SCORE_BUFFERS:-1]
    y_ref = refs[-1]
    n_ctx = kac_ref.shape[0] * TILE
    n_keys = n_ctx + (kal_ref.shape[0] * TILE if with_latent_keys else 0)

    def rows_of(index, size):
        return slice(index * size, (index + 1) * size)

    def value_chunk(c, rows, ctx_ref, lat_ref):
        if c * TILE < n_ctx:
            return ctx_ref[rows, c * TILE:(c + 1) * TILE]
        return lat_ref[rows, c * TILE - n_ctx:(c + 1) * TILE - n_ctx]

    def a_scores(hd, s_ref):
        q = qa_ref[rows_of(hd, HEAD_DIM), :]
        zeros = jnp.zeros_like(q)
        q = jnp.concatenate([q, zeros] if hd < GQA_GROUP else [zeros, q], axis=0)
        parts = [kac_ref[...].reshape(n_ctx, A_KV)]
        if with_latent_keys:
            parts.append(kal_ref[...].reshape(n_keys - n_ctx, A_KV))
        return _scores(parts, q, s_ref)

    def a_output(hd, s_ref, m):
        v_rows = rows_of(hd // GQA_GROUP, HEAD_DIM)
        out = _weighted_values(s_ref, n_keys, m,
                               lambda c: value_chunk(c, v_rows, vac_ref, val_ref if with_latent_keys else None),
                               HEAD_DIM)
        y_ref[rows_of(hd, HEAD_DIM), :] = out.astype(BF16)

    def b_scores(hd, s_ref):
        parts = [kbc_ref[:, hd].reshape(n_ctx, LANES)]
        if with_latent_keys:
            parts.append(kbl_ref[:, hd].reshape(n_keys - n_ctx, LANES))
        return _scores(parts, qb_ref[rows_of(hd, LANES), :], s_ref)

    def b_output(hd, s_ref, m):
        v_rows = rows_of(hd, B_V)
        out = _weighted_values(s_ref, n_keys, m,
                               lambda c: value_chunk(c, v_rows, vbc_ref, vbl_ref if with_latent_keys else None),
                               B_V)
        y_ref[rows_of(N_HEADS_HALF + hd, B_V), :] = out.astype(BF16)

    heads = ([(a_scores, a_output, hd) for hd in range(N_HEADS_HALF)]
             + [(b_scores, b_output, hd) for hd in range(N_HEADS_HALF)])
    lookahead = len(s_refs) - 1
    maxes = {i: heads[i][0](heads[i][2], s_refs[i % len(s_refs)]) for i in range(lookahead)}
    for i, (_, output, hd) in enumerate(heads):
        j = i + lookahead
        if j < len(heads):
            maxes[j] = heads[j][0](heads[j][2], s_refs[j % len(s_refs)])
        output(hd, s_refs[i % len(s_refs)], maxes.pop(i))

    o_ref[...] = h_ref[...] + mod_ref[2:3, :] * _dot_tn(y_ref[...], w_out_ref[...])


def _even_attn(h, mod, shared_mod, q, ctx_kv, lat_kv, w_out):
    bsz, seq, d = h.shape
    qa, qb = q
    with_latent_keys = lat_kv is not None
    kv = list(ctx_kv) + (list(lat_kv) if with_latent_keys else [])
    n_keys = sum(arr.shape[1] for arr in (ctx_kv[0],) + ((lat_kv[0],) if with_latent_keys else ())) * TILE

    def tile_spec(arr):
        return pl.BlockSpec((None, None) + arr.shape[2:], lambda b, t: (b, t, 0, 0))

    return pl.pallas_call(
        functools.partial(_even_attn_kernel, with_latent_keys),
        grid=(bsz, seq // TILE),
        in_specs=[pl.BlockSpec((None, TILE, d), lambda b, t: (b, t, 0)), _mod_spec(mod, shared_mod),
                  tile_spec(qa), tile_spec(qb)] + [_batch_spec(arr) for arr in kv] + [_const_spec(w_out.shape)],
        out_specs=pl.BlockSpec((None, TILE, d), lambda b, t: (b, t, 0)),
        out_shape=jax.ShapeDtypeStruct((bsz, seq, d), F32),
        scratch_shapes=[pltpu.VMEM((n_keys, TILE), F32)] * N_SCORE_BUFFERS + [pltpu.VMEM((D_MODEL, TILE), BF16)],
        compiler_params=_params(),
        name="even_attn",
    )(h, mod, qa, qb, *kv, w_out)


def _conv_ffn_kernel(final_norm, *refs):
    h_ref, prev_ref, next_ref, mod_ref, g2_ref, w_up_ref, cw_ref, cb_ref, w_down_ref = refs[:9]
    if final_norm:
        gf_ref = refs[9]
    o_ref, hs0_ref, hs1_ref, act_ref = refs[-4:]
    t = pl.program_id(1)
    last_tile = pl.num_programs(1) - 1
    tile = h_ref.shape[0]
    n_chunks = w_up_ref.shape[0]
    g2 = g2_ref[...]
    shift, scale = mod_ref[3:4, :], mod_ref[4:5, :]
    has_prev = (t != 0).astype(F32)
    has_next = (t != last_tile).astype(F32)
    h = h_ref[...]
    z = jnp.concatenate([
        _norm_modulate(h, g2, shift, scale),
        _norm_modulate(prev_ref[...], g2, shift, scale) * has_prev,
        _norm_modulate(next_ref[...], g2, shift, scale) * has_next,
    ], axis=0).astype(BF16)

    hs_refs = (hs0_ref, hs1_ref)
    groups = tile // SUBLANES
    sub = lax.broadcasted_iota(jnp.int32, (groups, SUBLANES, 2 * FF_CHUNK), 1)

    def up(j):
        hs_refs[j % 2][...] = _dot(z, w_up_ref[j])

    def gate(j):
        hs_ref = hs_refs[j % 2]
        hs = hs_ref[0:tile, :].reshape(groups, SUBLANES, 2 * FF_CHUNK)
        before = hs_ref[tile:tile + SUBLANES, :].reshape(1, SUBLANES, 2 * FF_CHUNK)
        after = hs_ref[tile + SUBLANES:tile + 2 * SUBLANES, :].reshape(1, SUBLANES, 2 * FF_CHUNK)
        down = pltpu.roll(jnp.concatenate([before, hs], axis=0), 1, 1)
        up_ = pltpu.roll(jnp.concatenate([hs, after], axis=0), SUBLANES - 1, 1)
        below = jnp.where(sub == 0, down[:groups], down[1:])
        above = jnp.where(sub == SUBLANES - 1, up_[1:], up_[:groups])
        cw = cw_ref[j]
        conv = below * cw[0:1, :] + hs * cw[1:2, :] + above * cw[2:3, :] + cb_ref[j]
        a = conv[:, :, :FF_CHUNK]
        g = conv[:, :, FF_CHUNK:]
        act = (g * jax.nn.sigmoid(g) * a).reshape(tile, FF_CHUNK)
        act_ref[:, j * FF_CHUNK:(j + 1) * FF_CHUNK] = act.astype(BF16)

    up(0)
    for j in range(n_chunks):
        if j + 1 < n_chunks:
            up(j + 1)
        gate(j)
    out = h + mod_ref[5:6, :] * _dot(act_ref[...], w_down_ref[...])
    if final_norm:
        ms = jnp.mean(out * out, axis=-1, keepdims=True)
        out = out * lax.rsqrt(ms + NORM_EPS) * gf_ref[...]
    o_ref[...] = out


def _conv_ffn(h, mod, shared_mod, g2, w_up, conv_w, conv_b, w_down, step, final_g=None):
    bsz, seq, d = h.shape
    rows_per_step = step // SUBLANES
    last_block = seq // SUBLANES - 1
    final_norm = final_g is not None

    def weight_spec(w):
        return pl.BlockSpec(w.shape, lambda b, t, _n=w.ndim: (0,) * _n, pipeline_mode=pl.Buffered(1))

    in_specs = [
        pl.BlockSpec((None, step, d), lambda b, t: (b, t, 0)),
        pl.BlockSpec((None, SUBLANES, d), lambda b, t: (b, jnp.maximum(t * rows_per_step - 1, 0), 0)),
        pl.BlockSpec((None, SUBLANES, d),
                     lambda b, t: (b, jnp.minimum((t + 1) * rows_per_step, last_block), 0)),
        _mod_spec(mod, shared_mod),
        _const_spec(g2.shape), weight_spec(w_up), _const_spec(conv_w.shape),
        _const_spec(conv_b.shape), weight_spec(w_down),
    ]
    args = [h, h, h, mod, g2, w_up, conv_w, conv_b, w_down]
    if final_norm:
        in_specs.append(_const_spec(final_g.shape))
        args.append(final_g)
    return pl.pallas_call(
        functools.partial(_conv_ffn_kernel, final_norm),
        grid=(bsz, seq // step),
        in_specs=in_specs,
        out_specs=pl.BlockSpec((None, step, d), lambda b, t: (b, t, 0)),
        out_shape=jax.ShapeDtypeStruct((bsz, seq, d), F32),
        scratch_shapes=[pltpu.VMEM((step + 2 * SUBLANES, 2 * FF_CHUNK), F32),
                        pltpu.VMEM((step + 2 * SUBLANES, 2 * FF_CHUNK), F32),
                        pltpu.VMEM((step, D_FF), BF16)],
        compiler_params=_params(),
        name="conv_ffn",
    )(*args)


def _gelu(x):
    return 0.5 * x * (1.0 + jnp.tanh(0.7978845608028654 * (x + 0.044715 * (x * x * x))))


def _odd_proj_kernel(h_ref, mod_ref, g1_ref, w_row_ref, w_t_ref, ln_g_ref, ln_b_ref, ws_ref, bs_ref,
                     cos_ht_ref, sin_ht_ref, cos_k_ref, sin_k_ref,
                     yc_ref, qd_ref, kd_ref, vd_ref):
    tile = TILE
    chunks = tile // C_CHUNK
    lane = lax.broadcasted_iota(jnp.int32, (tile, LANES), 1)
    gw = C_WIDTH // C_GROUPS
    for s in range(h_ref.shape[0] // tile):
        z = _norm_modulate(h_ref[s * tile:(s + 1) * tile, :], g1_ref[...],
                           mod_ref[0:1, :], mod_ref[1:2, :]).astype(BF16)
        hr = _dot(z, w_row_ref[...])
        ht = _dot_nt(w_t_ref[...], z)

        v = _gelu(ht[C_WIDTH:2 * C_WIDTH])
        mu = jnp.mean(v, axis=0, keepdims=True)
        vc = v - mu
        var = jnp.mean(vc * vc, axis=0, keepdims=True)
        vn = vc * lax.rsqrt(var + NORM_EPS) * _lane_tile(ln_g_ref[...], tile) + _lane_tile(ln_b_ref[...], tile)
        vn = vn.astype(BF16)
        for g in range(C_GROUPS):
            for n in range(chunks):
                cols = slice(n * C_CHUNK, (n + 1) * C_CHUNK)
                rows = slice(g * gw, (g + 1) * gw)
                mixed = _dot(vn[rows, cols], ws_ref[g]) + bs_ref[g]
                yc_ref[s, rows, cols] = (_gelu(ht[rows, cols]) * mixed).astype(BF16)

        o = 2 * C_WIDTH
        qd = ht[o:o + A_Q].reshape(N_HEADS_HALF, HEAD_DIM, tile)
        qd = _rope_rows(qd, cos_ht_ref[s], sin_ht_ref[s]) * (HEAD_DIM ** -0.5 * LOG2_E)
        qd_ref[s] = qd.reshape(A_Q, tile).astype(BF16)
        kd = _rope_lanes(hr, cos_k_ref[s], sin_k_ref[s], HEAD_DIM // 2,
                         (lane & (HEAD_DIM - 1)) < HEAD_DIM // 2).astype(BF16)
        vd = ht[o + A_Q:o + A_Q + A_KV].astype(BF16)
        for n in range(chunks):
            kd_ref[s * chunks + n] = kd[n * C_CHUNK:(n + 1) * C_CHUNK, :]
            vd_ref[s * chunks + n] = vd[:, n * C_CHUNK:(n + 1) * C_CHUNK]


def _odd_proj(h, mod, g1, wts, tabs, step):
    bsz, seq, d = h.shape
    n_sub = step // TILE
    n_tiles = seq // TILE
    chunks = TILE // C_CHUNK

    def tiles_spec(rows):
        return pl.BlockSpec((None, n_sub, rows, TILE), lambda b, t: (b, t, 0, 0))

    chunk_spec = pl.BlockSpec((None, n_sub * chunks, C_CHUNK, C_CHUNK), lambda b, t: (b, t, 0, 0))
    in_specs = [pl.BlockSpec((None, step, d), lambda b, t: (b, t, 0)), _mod_spec(mod, False),
                _const_spec(g1.shape)]
    in_specs += [_const_spec(w.shape) for w in wts]
    in_specs += [pl.BlockSpec((n_sub,) + tb.shape[1:], lambda b, t: (t, 0, 0)) for tb in tabs]
    return pl.pallas_call(
        _odd_proj_kernel,
        grid=(bsz, seq // step),
        in_specs=in_specs,
        out_specs=[tiles_spec(C_WIDTH), tiles_spec(A_Q), chunk_spec, chunk_spec],
        out_shape=[jax.ShapeDtypeStruct((bsz, n_tiles, C_WIDTH, TILE), BF16),
                   jax.ShapeDtypeStruct((bsz, n_tiles, A_Q, TILE), BF16),
                   jax.ShapeDtypeStruct((bsz, n_tiles * chunks, C_CHUNK, A_KV), BF16),
                   jax.ShapeDtypeStruct((bsz, n_tiles * chunks, A_KV, C_CHUNK), BF16)],
        compiler_params=_params(),
        name="odd_proj",
    )(h, mod, g1, *wts, *tabs)


def _odd_kv_kernel(h_ref, mod_ref, g1_ref, w_k_ref, w_vt_ref, kd_ref, vd_ref):
    z = _norm_modulate(h_ref[...], g1_ref[...], mod_ref[0:1, :], mod_ref[1:2, :]).astype(BF16)
    kd = _dot(z, w_k_ref[...]).astype(BF16)
    vd = _dot_nt(w_vt_ref[...], z).astype(BF16)
    for n in range(h_ref.shape[0] // C_CHUNK):
        kd_ref[n] = kd[n * C_CHUNK:(n + 1) * C_CHUNK, :]
        vd_ref[n] = vd[:, n * C_CHUNK:(n + 1) * C_CHUNK]


def _odd_kv(h, mod, g1, w_k, w_vt):
    bsz, seq, d = h.shape
    chunks = seq // C_CHUNK
    chunk_spec = pl.BlockSpec((None, chunks, C_CHUNK, C_CHUNK), lambda b, t: (b, 0, 0, 0))
    return pl.pallas_call(
        _odd_kv_kernel,
        grid=(bsz, 1),
        in_specs=[pl.BlockSpec((None, seq, d), lambda b, t: (b, 0, 0)), _mod_spec(mod, True),
                  _const_spec(g1.shape), _const_spec(w_k.shape), _const_spec(w_vt.shape)],
        out_specs=[chunk_spec, chunk_spec],
        out_shape=[jax.ShapeDtypeStruct((bsz, chunks, C_CHUNK, A_KV), BF16),
                   jax.ShapeDtypeStruct((bsz, chunks, A_KV, C_CHUNK), BF16)],
        compiler_params=_params(),
        name="odd_kv",
    )(h, mod, g1, w_k, w_vt)


def _window_attn_kernel(sink_ref, h_ref, mod_ref, yc_ref, qd_ref, kdc_ref, vdc_ref, kdl_ref, vdl_ref, w_out_ref,
                        o_ref, s0_ref, s1_ref, bias_ref, y_ref):
    t = pl.program_id(1)
    tile = qd_ref.shape[1]
    ctx_chunks = kdc_ref.shape[0]
    lat_chunks = kdl_ref.shape[0]
    sub = tile // C_CHUNK
    win_chunks = sub + 2
    w0 = jnp.clip(t * sub - 1, 0, lat_chunks - win_chunks)
    n_ctx = ctx_chunks * C_CHUNK
    n_win = win_chunks * C_CHUNK
    pair = 2 * C_CHUNK
    s_refs = (s0_ref, s1_ref)

    k_ctx = kdc_ref[...].reshape(n_ctx, A_KV)
    k_win = kdl_ref[pl.ds(w0, win_chunks)].reshape(n_win, A_KV)
    keys = jnp.concatenate([k_ctx, k_win], axis=0)
    key_pos = w0 * C_CHUNK + lax.broadcasted_iota(jnp.int32, (n_win, tile), 0)
    qry_pos = t * tile + lax.broadcasted_iota(jnp.int32, (n_win, tile), 1)
    bias_ref[...] = jnp.where(jnp.abs(qry_pos - key_pos) <= WINDOW, 0.0, NEG_INF)
    zeros = jnp.zeros((HEAD_DIM, tile), BF16)
    ones = jnp.ones((BF16_ROWS, pair), BF16)

    def scores(hd):
        s_ref = s_refs[hd % 2]
        q = qd_ref[hd * HEAD_DIM:(hd + 1) * HEAD_DIM, :]
        q = jnp.concatenate([q, zeros] if hd < GQA_GROUP else [zeros, q], axis=0)
        s = _dot(keys, q)
        s_ctx = s[:n_ctx]
        s_win = s[n_ctx:] + bias_ref[...]
        s_ref[0:n_ctx, :] = s_ctx
        s_ref[n_ctx:, :] = s_win
        m = jnp.maximum(jnp.max(s_ctx.reshape(n_ctx // SUBLANES, SUBLANES, tile), axis=0),
                        jnp.max(s_win.reshape(n_win // SUBLANES, SUBLANES, tile), axis=0))
        return jnp.maximum(jnp.max(m, axis=0, keepdims=True), sink_ref[hd] * LOG2_E)

    def output(hd, m):
        s_ref = s_refs[hd % 2]
        rows = slice((hd // GQA_GROUP) * HEAD_DIM, (hd // GQA_GROUP + 1) * HEAD_DIM)
        acc = jnp.zeros((HEAD_DIM + BF16_ROWS, tile), F32)
        for c in range((n_ctx + n_win) // pair):
            p = jnp.exp2(s_ref[c * pair:(c + 1) * pair, :] - m)
            if 2 * c < ctx_chunks:
                v = [vdc_ref[2 * c, rows, :], vdc_ref[2 * c + 1, rows, :]]
            else:
                first = w0 + (2 * c - ctx_chunks)
                v = [vdl_ref[first, rows, :], vdl_ref[first + 1, rows, :]]
            v = jnp.concatenate([jnp.concatenate(v, axis=1), ones], axis=0)
            acc = acc + _dot(v, p.astype(BF16))
        l = acc[HEAD_DIM:HEAD_DIM + 1] + jnp.exp2(sink_ref[hd] * LOG2_E - m)
        y_ref[hd * HEAD_DIM:(hd + 1) * HEAD_DIM, :] = (acc[:HEAD_DIM] * (1.0 / l)).astype(BF16)

    m = scores(0)
    for hd in range(N_HEADS_HALF):
        m_next = scores(hd + 1) if hd + 1 < N_HEADS_HALF else None
        output(hd, m)
        m = m_next

    y = _dot_tn(yc_ref[...], w_out_ref[0:C_WIDTH, :]) + _dot_tn(y_ref[...], w_out_ref[C_WIDTH:, :])
    o_ref[...] = h_ref[...] + mod_ref[2:3, :] * y


def _window_attn(sink, h, mod, yc, qd, ctx_kv, lat_kv, w_out):
    bsz, seq, d = h.shape

    def tile_spec(rows):
        return pl.BlockSpec((None, None, rows, TILE), lambda b, t: (b, t, 0, 0))

    kv = list(ctx_kv) + list(lat_kv)
    return pl.pallas_call(
        _window_attn_kernel,
        grid=(bsz, seq // TILE),
        in_specs=[pl.BlockSpec(memory_space=pltpu.SMEM),
                  pl.BlockSpec((None, TILE, d), lambda b, t: (b, t, 0)), _mod_spec(mod, False),
                  tile_spec(C_WIDTH), tile_spec(A_Q)] + [_batch_spec(arr) for arr in kv]
        + [_const_spec(w_out.shape)],
        out_specs=pl.BlockSpec((None, TILE, d), lambda b, t: (b, t, 0)),
        out_shape=jax.ShapeDtypeStruct((bsz, seq, d), F32),
        scratch_shapes=[pltpu.VMEM((2 * TILE + 2 * C_CHUNK, TILE), F32)] * 2
        + [pltpu.VMEM((TILE + 2 * C_CHUNK, TILE), F32), pltpu.VMEM((A_Q, TILE), BF16)],
        compiler_params=_params(),
        name="window_attn",
    )(sink, h, mod, yc, qd, *kv, w_out)


def _rope_tables(seq):
    def base(dim):
        n_rows = seq // GRID_W
        rows = jnp.repeat(jnp.arange(n_rows), GRID_W).astype(F32)
        cols = jnp.tile(jnp.arange(GRID_W), n_rows).astype(F32)
        quarter = dim // 4
        inv_freq = ROPE_THETA ** (-jnp.arange(quarter, dtype=F32) / quarter)
        ang = jnp.concatenate([rows[:, None] * inv_freq, cols[:, None] * inv_freq], axis=-1)
        return jnp.cos(ang), jnp.sin(ang)

    n_tiles = seq // TILE

    def feature_major(tb):
        return tb.reshape(n_tiles, TILE, tb.shape[1]).transpose(0, 2, 1)

    def token_major(tb):
        return tb.reshape(n_tiles, TILE, LANES)

    cos_h, sin_h = base(HEAD_DIM)
    cos_r, sin_r = base(B_ROPE)
    cos_k = jnp.tile(cos_h, (1, LANES // (HEAD_DIM // 2)))
    sin_k = jnp.tile(jnp.concatenate([-sin_h, sin_h], axis=1), (1, LANES // HEAD_DIM))
    pad_l = jnp.zeros((seq, B_NOPE), F32)
    pad_r = jnp.zeros((seq, LANES - B_QK), F32)
    cos_kr = jnp.concatenate([pad_l, cos_r, cos_r, pad_r], axis=1)
    sin_kr = jnp.concatenate([pad_l, -sin_r, sin_r, pad_r], axis=1)
    return dict(cos_ht=feature_major(cos_h), sin_ht=feature_major(sin_h),
                cos_rt=feature_major(cos_r), sin_rt=feature_major(sin_r),
                cos_k=token_major(cos_k), sin_k=token_major(sin_k),
                cos_kr=token_major(cos_kr), sin_kr=token_major(sin_kr))


def _col(v):
    return jnp.broadcast_to(v.astype(F32)[:, None], (v.shape[0], LANES))


def _ffn_weights(w_up, conv_w, conv_b, w_down):
    n = D_FF // FF_CHUNK

    def pair(m):
        a = m[..., :D_FF].reshape(m.shape[:-1] + (n, FF_CHUNK))
        g = m[..., D_FF:].reshape(m.shape[:-1] + (n, FF_CHUNK))
        return jnp.moveaxis(jnp.concatenate([a, g], axis=-1), -2, 0)

    return pair(w_up).astype(BF16), pair(conv_w), pair(conv_b[None, :]), w_down.astype(BF16)


def kernel(x, c, ctx, c_ctx, mod_w, mod_b, norm1_g, norm2_g, ev_w_in, ev_qa_g, ev_ka_g, ev_qlat_g, ev_w_q_up, ev_kvlat_g, ev_w_kv_up, ev_w_out, od_w_in, od_ln_g, od_ln_b, od_sgu_w, od_sgu_b, od_sink, od_w_out, ffn_up, ffn_conv_w, ffn_conv_b, ffn_down, final_g):
    bsz, seq, d = x.shape
    ctx_len = ctx.shape[1]
    assert d == D_MODEL and ctx_len % TILE == 0 and seq % LATENT_STEP == 0 and mod_w.shape[0] == 2

    n_rows = -(-(bsz + 1) // SUBLANES) * SUBLANES
    cc = jnp.zeros((n_rows, d), F32).at[:bsz].set(c).at[n_rows - 1].set(c_ctx)
    mods = _modulation(cc, mod_w, mod_b).reshape(2, n_rows, 6, d)
    tabs = _rope_tables(seq)

    w_in = ev_w_in[0]
    s0, s1, s2, s3, s4 = A_Q, A_Q + A_KV, A_Q + 2 * A_KV, A_Q + 2 * A_KV + B_Q_RANK, A_Q + 2 * A_KV + B_Q_RANK + B_KV_RANK
    w_kr = jnp.zeros((d, LANES), F32).at[:, B_NOPE:B_QK].set(w_in[:, s4:])
    w_row = jnp.concatenate([w_in[:, s0:s1], w_in[:, s3:s4], w_kr], axis=1).astype(BF16)
    w_t = jnp.concatenate([w_in[:, :s0], w_in[:, s1:s2], w_in[:, s2:s3], w_in[:, s3:s4]], axis=1).T.astype(BF16)
    wq = ev_w_q_up[0].reshape(B_Q_RANK, N_HEADS_HALF, B_QK)
    wq = jnp.pad(wq, ((0, 0), (0, 0), (0, LANES - B_QK))).reshape(B_Q_RANK, N_HEADS_HALF * LANES).T.astype(BF16)
    wkv = ev_w_kv_up[0].reshape(B_KV_RANK, N_HEADS_HALF, B_NOPE + B_V)
    wk = jnp.pad(wkv[:, :, :B_NOPE], ((0, 0), (0, 0), (0, LANES - B_NOPE)))
    wk = wk.reshape(B_KV_RANK, N_HEADS_HALF * LANES).astype(BF16)
    wv = wkv[:, :, B_NOPE:].reshape(B_KV_RANK, N_HEADS_HALF * B_V).T.astype(BF16)
    even_wts = [w_row, w_t, wq, wk, wv,
                _col(ev_qa_g[0] * (HEAD_DIM ** -0.5 * LOG2_E)), jnp.tile(ev_ka_g[0], KV_HEADS)[None, :],
                _col(ev_qlat_g[0]), _col(ev_kvlat_g[0]), ev_kvlat_g[0][None, :]]
    even_tabs = [tabs[k] for k in ("cos_ht", "sin_ht", "cos_rt", "sin_rt", "cos_k", "sin_k", "cos_kr", "sin_kr")]
    g1 = norm1_g[0][None, :]
    qa_c, ka_c, va_c, qb_c, kb_c, vb_c = _even_proj(ctx, mods[0], True, g1, even_wts, None, ctx_len)
    qa_l, ka_l, va_l, qb_l, kb_l, vb_l = _even_proj(x, mods[0], False, g1, even_wts, even_tabs, LATENT_STEP)
    ctx_kv = (ka_c, va_c, kb_c, vb_c)
    w_out = ev_w_out[0].astype(BF16)
    hc = _even_attn(ctx, mods[0], True, (qa_c, qb_c), ctx_kv, None, w_out)
    hl = _even_attn(x, mods[0], False, (qa_l, qb_l), ctx_kv, (ka_l, va_l, kb_l, vb_l), w_out)
    ffn0 = _ffn_weights(ffn_up[0], ffn_conv_w[0], ffn_conv_b[0], ffn_down[0])
    g2 = norm2_g[0][None, :]
    hc = _conv_ffn(hc, mods[0], True, g2, *ffn0, step=ctx_len)
    hl = _conv_ffn(hl, mods[0], False, g2, *ffn0, step=LATENT_STEP)

    w_in = od_w_in[0]
    o2, o3 = 2 * C_WIDTH + A_Q, 2 * C_WIDTH + A_Q + A_KV
    w_row = w_in[:, o2:o3].astype(BF16)
    w_t = jnp.concatenate([w_in[:, :o2], w_in[:, o3:]], axis=1).T.astype(BF16)
    ws_t = od_sgu_w[0].transpose(0, 2, 1).astype(BF16)
    odd_wts = [w_row, w_t, _col(od_ln_g[0]), _col(od_ln_b[0]), ws_t, od_sgu_b[0][:, None, :]]
    odd_tabs = [tabs[k] for k in ("cos_ht", "sin_ht", "cos_k", "sin_k")]
    g1 = norm1_g[1][None, :]
    kd_c, vd_c = _odd_kv(hc, mods[1], g1, w_row, w_in[:, o3:].T.astype(BF16))
    yc, qd, kd_l, vd_l = _odd_proj(hl, mods[1], g1, odd_wts, odd_tabs, TILE)
    hl = _window_attn(od_sink[0], hl, mods[1], yc, qd, (kd_c, vd_c), (kd_l, vd_l), od_w_out[0].astype(BF16))
    return _conv_ffn(hl, mods[1], False, norm2_g[1][None, :],
                     *_ffn_weights(ffn_up[1], ffn_conv_w[1], ffn_conv_b[1], ffn_down[1]),
                     step=LATENT_STEP, final_g=final_g[None, :])
```

```python
import functools

import jax
import jax.numpy as jnp
import numpy as np
from jax import lax
from jax.experimental import pallas as pl
from jax.experimental.pallas import tpu as pltpu

F32 = jnp.float32
BF16 = jnp.bfloat16

D_MODEL = 1024
HEAD_DIM = 64
GRID_W = 64
ROPE_THETA = 10000.0
NORM_EPS = 1e-6
NEG_INF = -1e30
LOG2_E = 1.4426950408889634
N_HEADS_HALF = 8
KV_HEADS = 2
GQA_GROUP = N_HEADS_HALF // KV_HEADS
A_Q = N_HEADS_HALF * HEAD_DIM
A_KV = KV_HEADS * HEAD_DIM
B_Q_RANK = 256
B_KV_RANK = 128
B_NOPE = 64
B_ROPE = 32
B_V = 64
B_QK = B_NOPE + B_ROPE
C_WIDTH = 512
C_GROUPS = 8
C_CHUNK = 128
WINDOW = 128
D_FF = 2816
LANES = 128
SUBLANES = 8
BF16_ROWS = 16
TILE = 256
LATENT_STEP = 512
FF_CHUNK = 256
N_SCORE_BUFFERS = 3
VMEM_LIMIT = 56 * 1024 * 1024


def _dot(a, b):
    return jnp.dot(a, b, preferred_element_type=F32)


def _dot_nt(a, b):
    return lax.dot_general(a, b, (((1,), (1,)), ((), ())), preferred_element_type=F32)


def _dot_tn(a, b):
    return lax.dot_general(a, b, (((0,), (0,)), ((), ())), preferred_element_type=F32)


def _lane_tile(v, width):
    return jnp.concatenate([v] * (width // LANES), axis=1)


def _norm_modulate(h, g, shift, scale):
    ms = jnp.mean(h * h, axis=-1, keepdims=True)
    y = h * lax.rsqrt(ms + NORM_EPS) * g
    return y * (1.0 + scale) + shift


def _rope_rows(x, cos, sin):
    half = cos.shape[0]
    x1 = x[..., :half, :]
    x2 = x[..., half:, :]
    return jnp.concatenate([x1 * cos - x2 * sin, x1 * sin + x2 * cos], axis=-2)


def _rope_lanes(x, cos, sin_signed, half, first_half_mask):
    up = pltpu.roll(x, LANES - half, 1)
    down = pltpu.roll(x, half, 1)
    return x * cos + jnp.where(first_half_mask, up, down) * sin_signed


def _const_spec(shape):
    nd = len(shape)
    return pl.BlockSpec(shape, lambda b, t, _n=nd: (0,) * _n)


def _batch_spec(arr):
    nd = arr.ndim - 1
    return pl.BlockSpec((None,) + arr.shape[1:], lambda b, t, _n=nd: (b,) + (0,) * _n)


def _mod_spec(mod, shared_row):
    n_rows, _, d = mod.shape
    if shared_row:
        return pl.BlockSpec((None, 6, d), lambda b, t: (n_rows - 1, 0, 0))
    return pl.BlockSpec((None, 6, d), lambda b, t: (b, 0, 0))


def _params():
    return pltpu.CompilerParams(dimension_semantics=("parallel", "arbitrary"), vmem_limit_bytes=VMEM_LIMIT)


def _mod_kernel(c_ref, w_ref, b_ref, o_ref):
    c = c_ref[...]
    a = c * jax.nn.sigmoid(c)
    o_ref[...] = jnp.dot(a, w_ref[...], preferred_element_type=F32,
                         precision=lax.Precision.HIGHEST) + b_ref[...]


def _modulation(cc, mod_w, mod_b):
    depth, d, n = mod_w.shape
    rows = cc.shape[0]
    bn = 1024
    return pl.pallas_call(
        _mod_kernel,
        grid=(depth, n // bn),
        in_specs=[
            pl.BlockSpec((rows, d), lambda l, j: (0, 0)),
            pl.BlockSpec((None, d, bn), lambda l, j: (l, 0, j)),
            pl.BlockSpec((None, 1, bn), lambda l, j: (l, 0, j)),
        ],
        out_specs=pl.BlockSpec((None, rows, bn), lambda l, j: (l, 0, j)),
        out_shape=jax.ShapeDtypeStruct((depth, rows, n), F32),
        compiler_params=pltpu.CompilerParams(dimension_semantics=("arbitrary", "arbitrary")),
        name="modulation",
    )(cc, mod_w, mod_b.reshape(depth, 1, n))


def _even_proj_kernel(rotate, *refs):
    (h_ref, mod_ref, g1_ref, w_row_ref, w_t_ref, wq_up_ref, wk_up_ref, wv_up_ref,
     qa_g_ref, ka_g_ref, qlat_g_ref, kvlat_gc_ref, kvlat_gr_ref) = refs[:13]
    if rotate:
        (cos_ht_ref, sin_ht_ref, cos_rt_ref, sin_rt_ref,
         cos_k_ref, sin_k_ref, cos_kr_ref, sin_kr_ref) = refs[13:21]
    qa_ref, ka_ref, va_ref, qb_ref, kb_ref, vb_ref = refs[-6:]
    tile = TILE
    lane = lax.broadcasted_iota(jnp.int32, (tile, LANES), 1)
    head0 = lane < HEAD_DIM
    for s in range(h_ref.shape[0] // tile):
        rows = slice(s * tile, (s + 1) * tile)
        z = _norm_modulate(h_ref[rows, :], g1_ref[...], mod_ref[0:1, :], mod_ref[1:2, :]).astype(BF16)
        o = A_Q + A_KV
        ht_cq = _dot_nt(w_t_ref[o:o + B_Q_RANK, :], z)
        ht_ckv = _dot_nt(w_t_ref[o + B_Q_RANK:, :], z)
        hr = _dot(z, w_row_ref[...])
        ht_a = _dot_nt(w_t_ref[0:o, :], z)

        qa = ht_a[0:A_Q].reshape(N_HEADS_HALF, HEAD_DIM, tile)
        r = lax.rsqrt(jnp.mean(qa * qa, axis=1, keepdims=True) + NORM_EPS)
        qa = qa * r * _lane_tile(qa_g_ref[...], tile)
        if rotate:
            qa = _rope_rows(qa, cos_ht_ref[s], sin_ht_ref[s])
        qa_ref[s] = qa.reshape(A_Q, tile).astype(BF16)

        ka = hr[:, 0:A_KV]
        sq = ka * ka
        ss0 = jnp.sum(jnp.where(head0, sq, 0.0), axis=-1, keepdims=True)
        ss1 = jnp.sum(jnp.where(head0, 0.0, sq), axis=-1, keepdims=True)
        r = lax.rsqrt(jnp.where(head0, ss0, ss1) * (1.0 / HEAD_DIM) + NORM_EPS)
        ka = ka * r * ka_g_ref[...]
        if rotate:
            ka = _rope_lanes(ka, cos_k_ref[s], sin_k_ref[s], HEAD_DIM // 2,
                             (lane & (HEAD_DIM - 1)) < HEAD_DIM // 2)
        ka_ref[s] = ka.astype(BF16)

        va_ref[:, rows] = ht_a[A_Q:A_Q + A_KV].astype(BF16)

        cq = ht_cq
        r = lax.rsqrt(jnp.mean(cq * cq, axis=0, keepdims=True) + NORM_EPS)
        cq = (cq * r * _lane_tile(qlat_g_ref[...], tile)).astype(BF16)
        qb = _dot(wq_up_ref[...], cq).reshape(N_HEADS_HALF, LANES, tile)
        if rotate:
            q_rope = _rope_rows(qb[:, B_NOPE:B_QK, :], cos_rt_ref[s], sin_rt_ref[s])
            qb = jnp.concatenate([qb[:, :B_NOPE, :], q_rope, qb[:, B_QK:, :]], axis=1)
        qb = qb * (B_QK ** -0.5 * LOG2_E)
        qb_ref[s] = qb.reshape(N_HEADS_HALF * LANES, tile).astype(BF16)

        ckv = hr[:, A_KV:A_KV + B_KV_RANK]
        r = lax.rsqrt(jnp.mean(ckv * ckv, axis=-1, keepdims=True) + NORM_EPS)
        ckv = (ckv * r * kvlat_gr_ref[...]).astype(BF16)
        k_nope = _dot(ckv, wk_up_ref[...])
        kr = hr[:, A_KV + B_KV_RANK:A_KV + B_KV_RANK + LANES]
        if rotate:
            kr = _rope_lanes(kr, cos_kr_ref[s], sin_kr_ref[s], B_ROPE // 2, lane < B_NOPE + B_ROPE // 2)
        for hd in range(N_HEADS_HALF):
            kb_ref[s, hd] = (k_nope[:, hd * LANES:(hd + 1) * LANES] + kr).astype(BF16)

        ckvt = ht_ckv
        r = lax.rsqrt(jnp.mean(ckvt * ckvt, axis=0, keepdims=True) + NORM_EPS)
        ckvt = (ckvt * r * _lane_tile(kvlat_gc_ref[...], tile)).astype(BF16)
        vb_ref[:, rows] = _dot(wv_up_ref[...], ckvt).astype(BF16)


def _even_proj(h, mod, shared_mod, g1, wts, tabs, step):
    bsz, seq, d = h.shape
    n_sub = step // TILE
    n_tiles = seq // TILE
    rotate = tabs is not None

    def tiles_spec(*dims):
        nd = len(dims)
        return pl.BlockSpec((None, n_sub) + dims, lambda b, t, _n=nd: (b, t) + (0,) * _n)

    def seq_spec(rows):
        return pl.BlockSpec((None, rows, step), lambda b, t: (b, 0, t))

    in_specs = [pl.BlockSpec((None, step, d), lambda b, t: (b, t, 0)), _mod_spec(mod, shared_mod),
                _const_spec(g1.shape)]
    in_specs += [_const_spec(w.shape) for w in wts]
    if rotate:
        in_specs += [pl.BlockSpec((n_sub,) + tb.shape[1:], lambda b, t: (t, 0, 0)) for tb in tabs]
    out_specs = [
        tiles_spec(A_Q, TILE), tiles_spec(TILE, A_KV), seq_spec(A_KV),
        tiles_spec(N_HEADS_HALF * LANES, TILE), tiles_spec(N_HEADS_HALF, TILE, LANES),
        seq_spec(N_HEADS_HALF * B_V),
    ]
    out_shapes = [
        (bsz, n_tiles, A_Q, TILE), (bsz, n_tiles, TILE, A_KV), (bsz, A_KV, seq),
        (bsz, n_tiles, N_HEADS_HALF * LANES, TILE), (bsz, n_tiles, N_HEADS_HALF, TILE, LANES),
        (bsz, N_HEADS_HALF * B_V, seq),
    ]
    return pl.pallas_call(
        functools.partial(_even_proj_kernel, rotate),
        grid=(bsz, seq // step),
        in_specs=in_specs,
        out_specs=out_specs,
        out_shape=[jax.ShapeDtypeStruct(shp, BF16) for shp in out_shapes],
        compiler_params=_params(),
        name="even_proj",
    )(h, mod, g1, *wts, *(tabs if rotate else ()))


def _scores(key_parts, q, s_ref):
    tile = q.shape[1]
    off = 0
    m = None
    for keys in key_parts:
        n = keys.shape[0]
        s = _dot(keys, q)
        s_ref[off:off + n, :] = s
        part = jnp.max(s.reshape(n // SUBLANES, SUBLANES, tile), axis=0)
        m = part if m is None else jnp.maximum(m, part)
        off += n
    return jnp.max(m, axis=0, keepdims=True)


def _weighted_values(s_ref, n_keys, m, value_chunk, dv):
    tile = s_ref.shape[1]
    groups = TILE // SUBLANES
    l = jnp.zeros((SUBLANES, tile), F32)
    acc = jnp.zeros((dv, tile), F32)
    for c in range(n_keys // TILE):
        p = jnp.exp2(s_ref[c * TILE:(c + 1) * TILE, :] - m)
        l = l + jnp.sum(p.reshape(groups, SUBLANES, tile), axis=0)
        acc = acc + _dot(value_chunk(c), p.astype(BF16))
    l = jnp.sum(l, axis=0, keepdims=True)
    return acc * (1.0 / l)


def _even_attn_kernel(with_latent_keys, *refs):
    h_ref, mod_ref, qa_ref, qb_ref, kac_ref, vac_ref, kbc_ref, vbc_ref = refs[:8]
    if with_latent_keys:
        kal_ref, val_ref, kbl_ref, vbl_ref = refs[8:12]
    w_out_ref, o_ref = refs[-3 - N_SCORE_BUFFERS:-1 - N_SCORE_BUFFERS]
    s_refs = refs[-1 - N_SCORE_BUFFERS:-1]
    y_ref = refs[-1]
    n_ctx = kac_ref.shape[0] * TILE
    n_keys = n_ctx + (kal_ref.shape[0] * TILE if with_latent_keys else 0)

    def rows_of(index, size):
        return slice(index * size, (index + 1) * size)

    def value_chunk(c, rows, ctx_ref, lat_ref):
        if c * TILE < n_ctx:
            return ctx_ref[rows, c * TILE:(c + 1) * TILE]
        return lat_ref[rows, c * TILE - n_ctx:(c + 1) * TILE - n_ctx]

    def a_scores(qt, hd, s_ref):
        q = qa_ref[qt, rows_of(hd, HEAD_DIM), :]
        zeros = jnp.zeros_like(q)
        q = jnp.concatenate([q, zeros] if hd < GQA_GROUP else [zeros, q], axis=0)
        parts = [kac_ref[...].reshape(n_ctx, A_KV)]
        if with_latent_keys:
            parts.append(kal_ref[...].reshape(n_keys - n_ctx, A_KV))
        return _scores(parts, q, s_ref)

    def a_output(qt, hd, s_ref, m):
        v_rows = rows_of(hd // GQA_GROUP, HEAD_DIM)
        out = _weighted_values(s_ref, n_keys, m,
                               lambda c: value_chunk(c, v_rows, vac_ref, val_ref if with_latent_keys else None),
                               HEAD_DIM)
        y_ref[qt, rows_of(hd, HEAD_DIM), :] = out.astype(BF16)

    def b_scores(qt, hd, s_ref):
        parts = [kbc_ref[:, hd].reshape(n_ctx, LANES)]
        if with_latent_keys:
            parts.append(kbl_ref[:, hd].reshape(n_keys - n_ctx, LANES))
        return _scores(parts, qb_ref[qt, rows_of(hd, LANES), :], s_ref)

    def b_output(qt, hd, s_ref, m):
        v_rows = rows_of(hd, B_V)
        out = _weighted_values(s_ref, n_keys, m,
                               lambda c: value_chunk(c, v_rows, vbc_ref, vbl_ref if with_latent_keys else None),
                               B_V)
        y_ref[qt, rows_of(N_HEADS_HALF + hd, B_V), :] = out.astype(BF16)

    heads = []
    for qt in range(qa_ref.shape[0]):
        heads += [(a_scores, a_output, qt, hd) for hd in range(N_HEADS_HALF)]
        heads += [(b_scores, b_output, qt, hd) for hd in range(N_HEADS_HALF)]
    lookahead = len(s_refs) - 1
    maxes = {i: heads[i][0](heads[i][2], heads[i][3], s_refs[i % len(s_refs)]) for i in range(lookahead)}
    for i, (_, output, qt, hd) in enumerate(heads):
        j = i + lookahead
        if j < len(heads):
            maxes[j] = heads[j][0](heads[j][2], heads[j][3], s_refs[j % len(s_refs)])
        output(qt, hd, s_refs[i % len(s_refs)], maxes.pop(i))
        if i + 1 == len(heads) or heads[i + 1][2] != qt:
            rows = rows_of(qt, TILE)
            o_ref[rows, :] = h_ref[rows, :] + mod_ref[2:3, :] * _dot_tn(y_ref[qt], w_out_ref[...])


def _even_attn(h, mod, shared_mod, q, ctx_kv, lat_kv, w_out, step):
    bsz, seq, d = h.shape
    n_q = step // TILE
    qa, qb = q
    with_latent_keys = lat_kv is not None
    kv = list(ctx_kv) + (list(lat_kv) if with_latent_keys else [])
    n_keys = sum(arr.shape[1] for arr in (ctx_kv[0],) + ((lat_kv[0],) if with_latent_keys else ())) * TILE

    def tile_spec(arr):
        return pl.BlockSpec((None, n_q) + arr.shape[2:], lambda b, t: (b, t, 0, 0))

    return pl.pallas_call(
        functools.partial(_even_attn_kernel, with_latent_keys),
        grid=(bsz, seq // step),
        in_specs=[pl.BlockSpec((None, step, d), lambda b, t: (b, t, 0)), _mod_spec(mod, shared_mod),
                  tile_spec(qa), tile_spec(qb)] + [_batch_spec(arr) for arr in kv] + [_const_spec(w_out.shape)],
        out_specs=pl.BlockSpec((None, step, d), lambda b, t: (b, t, 0)),
        out_shape=jax.ShapeDtypeStruct((bsz, seq, d), F32),
        scratch_shapes=[pltpu.VMEM((n_keys, TILE), F32)] * N_SCORE_BUFFERS
        + [pltpu.VMEM((n_q, D_MODEL, TILE), BF16)],
        compiler_params=_params(),
        name="even_attn",
    )(h, mod, qa, qb, *kv, w_out)


def _conv_ffn_kernel(final_norm, *refs):
    h_ref, prev_ref, next_ref, mod_ref, g2_ref, w_up_ref, cw_ref, cb_ref, w_down_ref = refs[:9]
    if final_norm:
        gf_ref = refs[9]
    o_ref, hsa0_ref, hsg0_ref, hsa1_ref, hsg1_ref, act_ref = refs[-6:]
    t = pl.program_id(1)
    last_tile = pl.num_programs(1) - 1
    tile = h_ref.shape[0]
    n_chunks = D_FF // FF_CHUNK
    g2 = g2_ref[...]
    shift, scale = mod_ref[3:4, :], mod_ref[4:5, :]
    has_prev = (t != 0).astype(F32)
    has_next = (t != last_tile).astype(F32)
    h = h_ref[...]
    z = jnp.concatenate([
        _norm_modulate(h, g2, shift, scale),
        _norm_modulate(prev_ref[...], g2, shift, scale) * has_prev,
        _norm_modulate(next_ref[...], g2, shift, scale) * has_next,
    ], axis=0).astype(BF16)

    hs_refs = ((hsa0_ref, hsg0_ref), (hsa1_ref, hsg1_ref))
    groups = tile // SUBLANES
    sub = lax.broadcasted_iota(jnp.int32, (groups, SUBLANES, FF_CHUNK), 1)

    def up(j):
        for hs_ref, col in zip(hs_refs[j % 2], (j * FF_CHUNK, D_FF + j * FF_CHUNK)):
            hs_ref[...] = _dot(z, w_up_ref[:, col:col + FF_CHUNK])

    def conv(hs_ref, col):
        hs = hs_ref[0:tile, :].reshape(groups, SUBLANES, FF_CHUNK)
        before = hs_ref[tile:tile + SUBLANES, :].reshape(1, SUBLANES, FF_CHUNK)
        after = hs_ref[tile + SUBLANES:tile + 2 * SUBLANES, :].reshape(1, SUBLANES, FF_CHUNK)
        down = pltpu.roll(jnp.concatenate([before, hs], axis=0), 1, 1)
        up_ = pltpu.roll(jnp.concatenate([hs, after], axis=0), SUBLANES - 1, 1)
        below = jnp.where(sub == 0, down[:groups], down[1:])
        above = jnp.where(sub == SUBLANES - 1, up_[1:], up_[:groups])
        cw = cw_ref[:, col:col + FF_CHUNK]
        return below * cw[0:1, :] + hs * cw[1:2, :] + above * cw[2:3, :] + cb_ref[:, col:col + FF_CHUNK]

    def gate(j):
        a = conv(hs_refs[j % 2][0], j * FF_CHUNK)
        g = conv(hs_refs[j % 2][1], D_FF + j * FF_CHUNK)
        act = (g * jax.nn.sigmoid(g) * a).reshape(tile, FF_CHUNK)
        act_ref[:, j * FF_CHUNK:(j + 1) * FF_CHUNK] = act.astype(BF16)

    up(0)
    for j in range(n_chunks):
        if j + 1 < n_chunks:
            up(j + 1)
        gate(j)
    out = h + mod_ref[5:6, :] * _dot(act_ref[...], w_down_ref[...])
    if final_norm:
        ms = jnp.mean(out * out, axis=-1, keepdims=True)
        out = out * lax.rsqrt(ms + NORM_EPS) * gf_ref[...]
    o_ref[...] = out


def _conv_ffn(h, mod, shared_mod, g2, w_up, conv_w, conv_b, w_down, step, final_g=None):
    bsz, seq, d = h.shape
    rows_per_step = step // SUBLANES
    last_block = seq // SUBLANES - 1
    final_norm = final_g is not None

    def weight_spec(w):
        return pl.BlockSpec(w.shape, lambda b, t, _n=w.ndim: (0,) * _n, pipeline_mode=pl.Buffered(1))

    in_specs = [
        pl.BlockSpec((None, step, d), lambda b, t: (b, t, 0)),
        pl.BlockSpec((None, SUBLANES, d), lambda b, t: (b, jnp.maximum(t * rows_per_step - 1, 0), 0)),
        pl.BlockSpec((None, SUBLANES, d),
                     lambda b, t: (b, jnp.minimum((t + 1) * rows_per_step, last_block), 0)),
        _mod_spec(mod, shared_mod),
        _const_spec(g2.shape), weight_spec(w_up), _const_spec(conv_w.shape),
        _const_spec(conv_b.shape), weight_spec(w_down),
    ]
    args = [h, h, h, mod, g2, w_up, conv_w, conv_b, w_down]
    if final_norm:
        in_specs.append(_const_spec(final_g.shape))
        args.append(final_g)
    return pl.pallas_call(
        functools.partial(_conv_ffn_kernel, final_norm),
        grid=(bsz, seq // step),
        in_specs=in_specs,
        out_specs=pl.BlockSpec((None, step, d), lambda b, t: (b, t, 0)),
        out_shape=jax.ShapeDtypeStruct((bsz, seq, d), F32),
        scratch_shapes=[pltpu.VMEM((step + 2 * SUBLANES, FF_CHUNK), F32)] * 4
        + [pltpu.VMEM((step, D_FF), BF16)],
        compiler_params=_params(),
        name="conv_ffn",
    )(*args)


def _gelu(x):
    return 0.5 * x * (1.0 + jnp.tanh(0.7978845608028654 * (x + 0.044715 * (x * x * x))))


def _odd_proj_kernel(h_ref, mod_ref, g1_ref, w_row_ref, w_t_ref, ln_g_ref, ln_b_ref, ws_ref, bs_ref,
                     cos_ht_ref, sin_ht_ref, cos_k_ref, sin_k_ref,
                     yc_ref, qd_ref, kd_ref, vd_ref):
    tile = TILE
    chunks = tile // C_CHUNK
    lane = lax.broadcasted_iota(jnp.int32, (tile, LANES), 1)
    gw = C_WIDTH // C_GROUPS
    for s in range(h_ref.shape[0] // tile):
        z = _norm_modulate(h_ref[s * tile:(s + 1) * tile, :], g1_ref[...],
                           mod_ref[0:1, :], mod_ref[1:2, :]).astype(BF16)
        ht_v = _dot_nt(w_t_ref[C_WIDTH:2 * C_WIDTH, :], z)
        ht_u = _dot_nt(w_t_ref[0:C_WIDTH, :], z)
        ht_d = _dot_nt(w_t_ref[2 * C_WIDTH:, :], z)
        hr = _dot(z, w_row_ref[...])

        v = _gelu(ht_v)
        mu = jnp.mean(v, axis=0, keepdims=True)
        vc = v - mu
        var = jnp.mean(vc * vc, axis=0, keepdims=True)
        vn = vc * lax.rsqrt(var + NORM_EPS) * _lane_tile(ln_g_ref[...], tile) + _lane_tile(ln_b_ref[...], tile)
        vn = vn.astype(BF16)
        for g in range(C_GROUPS):
            for n in range(chunks):
                cols = slice(n * C_CHUNK, (n + 1) * C_CHUNK)
                rows = slice(g * gw, (g + 1) * gw)
                mixed = _dot(vn[rows, cols], ws_ref[g]) + bs_ref[g]
                yc_ref[s, rows, cols] = (_gelu(ht_u[rows, cols]) * mixed).astype(BF16)

        qd = ht_d[0:A_Q].reshape(N_HEADS_HALF, HEAD_DIM, tile)
        qd = _rope_rows(qd, cos_ht_ref[s], sin_ht_ref[s]) * (HEAD_DIM ** -0.5 * LOG2_E)
        qd_ref[s] = qd.reshape(A_Q, tile).astype(BF16)
        kd = _rope_lanes(hr, cos_k_ref[s], sin_k_ref[s], HEAD_DIM // 2,
                         (lane & (HEAD_DIM - 1)) < HEAD_DIM // 2).astype(BF16)
        vd = ht_d[A_Q:A_Q + A_KV].astype(BF16)
        for n in range(chunks):
            kd_ref[s * chunks + n] = kd[n * C_CHUNK:(n + 1) * C_CHUNK, :]
            vd_ref[s * chunks + n] = vd[:, n * C_CHUNK:(n + 1) * C_CHUNK]


def _odd_proj(h, mod, g1, wts, tabs, step):
    bsz, seq, d = h.shape
    n_sub = step // TILE
    n_tiles = seq // TILE
    chunks = TILE // C_CHUNK

    def tiles_spec(rows):
        return pl.BlockSpec((None, n_sub, rows, TILE), lambda b, t: (b, t, 0, 0))

    chunk_spec = pl.BlockSpec((None, n_sub * chunks, C_CHUNK, C_CHUNK), lambda b, t: (b, t, 0, 0))
    in_specs = [pl.BlockSpec((None, step, d), lambda b, t: (b, t, 0)), _mod_spec(mod, False),
                _const_spec(g1.shape)]
    in_specs += [_const_spec(w.shape) for w in wts]
    in_specs += [pl.BlockSpec((n_sub,) + tb.shape[1:], lambda b, t: (t, 0, 0)) for tb in tabs]
    return pl.pallas_call(
        _odd_proj_kernel,
        grid=(bsz, seq // step),
        in_specs=in_specs,
        out_specs=[tiles_spec(C_WIDTH), tiles_spec(A_Q), chunk_spec, chunk_spec],
        out_shape=[jax.ShapeDtypeStruct((bsz, n_tiles, C_WIDTH, TILE), BF16),
                   jax.ShapeDtypeStruct((bsz, n_tiles, A_Q, TILE), BF16),
                   jax.ShapeDtypeStruct((bsz, n_tiles * chunks, C_CHUNK, A_KV), BF16),
                   jax.ShapeDtypeStruct((bsz, n_tiles * chunks, A_KV, C_CHUNK), BF16)],
        compiler_params=_params(),
        name="odd_proj",
    )(h, mod, g1, *wts, *tabs)


def _odd_kv_kernel(h_ref, mod_ref, g1_ref, w_k_ref, w_vt_ref, kd_ref, vd_ref):
    z = _norm_modulate(h_ref[...], g1_ref[...], mod_ref[0:1, :], mod_ref[1:2, :]).astype(BF16)
    kd = _dot(z, w_k_ref[...]).astype(BF16)
    vd = _dot_nt(w_vt_ref[...], z).astype(BF16)
    for n in range(h_ref.shape[0] // C_CHUNK):
        kd_ref[n] = kd[n * C_CHUNK:(n + 1) * C_CHUNK, :]
        vd_ref[n] = vd[:, n * C_CHUNK:(n + 1) * C_CHUNK]


def _odd_kv(h, mod, g1, w_k, w_vt):
    bsz, seq, d = h.shape
    chunks = seq // C_CHUNK
    chunk_spec = pl.BlockSpec((None, chunks, C_CHUNK, C_CHUNK), lambda b, t: (b, 0, 0, 0))
    return pl.pallas_call(
        _odd_kv_kernel,
        grid=(bsz, 1),
        in_specs=[pl.BlockSpec((None, seq, d), lambda b, t: (b, 0, 0)), _mod_spec(mod, True),
                  _const_spec(g1.shape), _const_spec(w_k.shape), _const_spec(w_vt.shape)],
        out_specs=[chunk_spec, chunk_spec],
        out_shape=[jax.ShapeDtypeStruct((bsz, chunks, C_CHUNK, A_KV), BF16),
                   jax.ShapeDtypeStruct((bsz, chunks, A_KV, C_CHUNK), BF16)],
        compiler_params=_params(),
        name="odd_kv",
    )(h, mod, g1, w_k, w_vt)


def _window_attn_kernel(sink_ref, h_ref, mod_ref, yc_ref, qd_ref, kdc_ref, vdc_ref, kdl_ref, vdl_ref, w_out_ref,
                        o_ref, s0_ref, s1_ref, s2_ref, s3_ref, bias_ref, y_ref):
    t = pl.program_id(1)
    tile = qd_ref.shape[1]
    ctx_chunks = kdc_ref.shape[0]
    lat_chunks = kdl_ref.shape[0]
    sub = tile // C_CHUNK
    win_chunks = sub + 2
    w0 = jnp.clip(t * sub - 1, 0, lat_chunks - win_chunks)
    n_ctx = ctx_chunks * C_CHUNK
    n_win = win_chunks * C_CHUNK
    pair = 2 * C_CHUNK
    s_refs = (s0_ref, s1_ref, s2_ref, s3_ref)

    k_ctx = kdc_ref[...].reshape(n_ctx, A_KV)
    k_win = kdl_ref[pl.ds(w0, win_chunks)].reshape(n_win, A_KV)
    keys = jnp.concatenate([k_ctx, k_win], axis=0)
    key_pos = w0 * C_CHUNK + lax.broadcasted_iota(jnp.int32, (n_win, tile), 0)
    qry_pos = t * tile + lax.broadcasted_iota(jnp.int32, (n_win, tile), 1)
    bias_ref[...] = jnp.where(jnp.abs(qry_pos - key_pos) <= WINDOW, 0.0, NEG_INF)
    zeros = jnp.zeros((HEAD_DIM, tile), BF16)
    ones = jnp.ones((BF16_ROWS, pair), BF16)

    def scores(hd):
        s_ref = s_refs[hd % len(s_refs)]
        q = qd_ref[hd * HEAD_DIM:(hd + 1) * HEAD_DIM, :]
        q = jnp.concatenate([q, zeros] if hd < GQA_GROUP else [zeros, q], axis=0)
        s = _dot(keys, q)
        s_ctx = s[:n_ctx]
        s_win = s[n_ctx:] + bias_ref[...]
        s_ref[0:n_ctx, :] = s_ctx
        s_ref[n_ctx:, :] = s_win
        m = jnp.maximum(jnp.max(s_ctx.reshape(n_ctx // SUBLANES, SUBLANES, tile), axis=0),
                        jnp.max(s_win.reshape(n_win // SUBLANES, SUBLANES, tile), axis=0))
        return jnp.maximum(jnp.max(m, axis=0, keepdims=True), sink_ref[hd] * LOG2_E)

    def output(hd, m):
        s_ref = s_refs[hd % len(s_refs)]
        rows = slice((hd // GQA_GROUP) * HEAD_DIM, (hd // GQA_GROUP + 1) * HEAD_DIM)
        acc = jnp.zeros((HEAD_DIM + BF16_ROWS, tile), F32)
        for c in range((n_ctx + n_win) // pair):
            p = jnp.exp2(s_ref[c * pair:(c + 1) * pair, :] - m)
            if 2 * c < ctx_chunks:
                v = [vdc_ref[2 * c, rows, :], vdc_ref[2 * c + 1, rows, :]]
            else:
                first = w0 + (2 * c - ctx_chunks)
                v = [vdl_ref[first, rows, :], vdl_ref[first + 1, rows, :]]
            v = jnp.concatenate([jnp.concatenate(v, axis=1), ones], axis=0)
            acc = acc + _dot(v, p.astype(BF16))
        l = acc[HEAD_DIM:HEAD_DIM + 1] + jnp.exp2(sink_ref[hd] * LOG2_E - m)
        y_ref[hd * HEAD_DIM:(hd + 1) * HEAD_DIM, :] = (acc[:HEAD_DIM] * (1.0 / l)).astype(BF16)

    lookahead = len(s_refs) - 1
    maxes = {hd: scores(hd) for hd in range(lookahead)}
    for hd in range(N_HEADS_HALF):
        if hd + lookahead < N_HEADS_HALF:
            maxes[hd + lookahead] = scores(hd + lookahead)
        output(hd, maxes.pop(hd))

    y = _dot_tn(yc_ref[...], w_out_ref[0:C_WIDTH, :]) + _dot_tn(y_ref[...], w_out_ref[C_WIDTH:, :])
    o_ref[...] = h_ref[...] + mod_ref[2:3, :] * y


def _window_attn(sink, h, mod, yc, qd, ctx_kv, lat_kv, w_out):
    bsz, seq, d = h.shape

    def tile_spec(rows):
        return pl.BlockSpec((None, None, rows, TILE), lambda b, t: (b, t, 0, 0))

    kv = list(ctx_kv) + list(lat_kv)
    return pl.pallas_call(
        _window_attn_kernel,
        grid=(bsz, seq // TILE),
        in_specs=[pl.BlockSpec(memory_space=pltpu.SMEM),
                  pl.BlockSpec((None, TILE, d), lambda b, t: (b, t, 0)), _mod_spec(mod, False),
                  tile_spec(C_WIDTH), tile_spec(A_Q)] + [_batch_spec(arr) for arr in kv]
        + [_const_spec(w_out.shape)],
        out_specs=pl.BlockSpec((None, TILE, d), lambda b, t: (b, t, 0)),
        out_shape=jax.ShapeDtypeStruct((bsz, seq, d), F32),
        scratch_shapes=[pltpu.VMEM((2 * TILE + 2 * C_CHUNK, TILE), F32)] * 4
        + [pltpu.VMEM((TILE + 2 * C_CHUNK, TILE), F32), pltpu.VMEM((A_Q, TILE), BF16)],
        compiler_params=_params(),
        name="window_attn",
    )(sink, h, mod, yc, qd, *kv, w_out)


def _rope_tables(seq):
    def base(dim):
        n_rows = seq // GRID_W
        rows = np.repeat(np.arange(n_rows), GRID_W).astype(np.float64)
        cols = np.tile(np.arange(GRID_W), n_rows).astype(np.float64)
        quarter = dim // 4
        inv_freq = ROPE_THETA ** (-np.arange(quarter, dtype=np.float64) / quarter)
        ang = np.concatenate([rows[:, None] * inv_freq, cols[:, None] * inv_freq], axis=-1)
        return np.cos(ang), np.sin(ang)

    n_tiles = seq // TILE

    def feature_major(tb):
        return jnp.asarray(tb.reshape(n_tiles, TILE, tb.shape[1]).transpose(0, 2, 1), F32)

    def token_major(tb):
        return jnp.asarray(tb.reshape(n_tiles, TILE, LANES), F32)

    cos_h, sin_h = base(HEAD_DIM)
    cos_r, sin_r = base(B_ROPE)
    cos_k = np.tile(cos_h, (1, LANES // (HEAD_DIM // 2)))
    sin_k = np.tile(np.concatenate([-sin_h, sin_h], axis=1), (1, LANES // HEAD_DIM))
    pad_l = np.zeros((seq, B_NOPE))
    pad_r = np.zeros((seq, LANES - B_QK))
    cos_kr = np.concatenate([pad_l, cos_r, cos_r, pad_r], axis=1)
    sin_kr = np.concatenate([pad_l, -sin_r, sin_r, pad_r], axis=1)
    return dict(cos_ht=feature_major(cos_h), sin_ht=feature_major(sin_h),
                cos_rt=feature_major(cos_r), sin_rt=feature_major(sin_r),
                cos_k=token_major(cos_k), sin_k=token_major(sin_k),
                cos_kr=token_major(cos_kr), sin_kr=token_major(sin_kr))


def _col(v):
    return jnp.broadcast_to(v.astype(F32)[:, None], (v.shape[0], LANES))


def _ffn_weights(w_up, conv_w, conv_b, w_down):
    return w_up.astype(BF16), conv_w, conv_b[None, :], w_down.astype(BF16)


def kernel(x, c, ctx, c_ctx, mod_w, mod_b, norm1_g, norm2_g, ev_w_in, ev_qa_g, ev_ka_g, ev_qlat_g, ev_w_q_up, ev_kvlat_g, ev_w_kv_up, ev_w_out, od_w_in, od_ln_g, od_ln_b, od_sgu_w, od_sgu_b, od_sink, od_w_out, ffn_up, ffn_conv_w, ffn_conv_b, ffn_down, final_g):
    bsz, seq, d = x.shape
    ctx_len = ctx.shape[1]
    assert d == D_MODEL and ctx_len % TILE == 0 and seq % LATENT_STEP == 0 and mod_w.shape[0] == 2

    n_rows = -(-(bsz + 1) // SUBLANES) * SUBLANES
    cc = jnp.concatenate([c, jnp.zeros((n_rows - bsz - 1, d), F32), c_ctx[None, :]], axis=0)
    mods = _modulation(cc, mod_w, mod_b).reshape(2, n_rows, 6, d)
    tabs = _rope_tables(seq)

    w_in = ev_w_in[0]
    s0, s1, s2, s3, s4 = A_Q, A_Q + A_KV, A_Q + 2 * A_KV, A_Q + 2 * A_KV + B_Q_RANK, A_Q + 2 * A_KV + B_Q_RANK + B_KV_RANK
    w_kr = jnp.zeros((d, LANES), F32).at[:, B_NOPE:B_QK].set(w_in[:, s4:])
    w_row = jnp.concatenate([w_in[:, s0:s1], w_in[:, s3:s4], w_kr], axis=1).astype(BF16)
    w_t = jnp.concatenate([w_in[:, :s0], w_in[:, s1:s2], w_in[:, s2:s3], w_in[:, s3:s4]], axis=1).T.astype(BF16)
    wq = ev_w_q_up[0].reshape(B_Q_RANK, N_HEADS_HALF, B_QK)
    wq = jnp.pad(wq, ((0, 0), (0, 0), (0, LANES - B_QK))).reshape(B_Q_RANK, N_HEADS_HALF * LANES).T.astype(BF16)
    wkv = ev_w_kv_up[0].reshape(B_KV_RANK, N_HEADS_HALF, B_NOPE + B_V)
    wk = jnp.pad(wkv[:, :, :B_NOPE], ((0, 0), (0, 0), (0, LANES - B_NOPE)))
    wk = wk.reshape(B_KV_RANK, N_HEADS_HALF * LANES).astype(BF16)
    wv = wkv[:, :, B_NOPE:].reshape(B_KV_RANK, N_HEADS_HALF * B_V).T.astype(BF16)
    even_wts = [w_row, w_t, wq, wk, wv,
                _col(ev_qa_g[0] * (HEAD_DIM ** -0.5 * LOG2_E)), jnp.tile(ev_ka_g[0], KV_HEADS)[None, :],
                _col(ev_qlat_g[0]), _col(ev_kvlat_g[0]), ev_kvlat_g[0][None, :]]
    even_tabs = [tabs[k] for k in ("cos_ht", "sin_ht", "cos_rt", "sin_rt", "cos_k", "sin_k", "cos_kr", "sin_kr")]
    g1 = norm1_g[0][None, :]
    qa_c, ka_c, va_c, qb_c, kb_c, vb_c = _even_proj(ctx, mods[0], True, g1, even_wts, None, ctx_len)
    qa_l, ka_l, va_l, qb_l, kb_l, vb_l = _even_proj(x, mods[0], False, g1, even_wts, even_tabs, LATENT_STEP)
    ctx_kv = (ka_c, va_c, kb_c, vb_c)
    w_out = ev_w_out[0].astype(BF16)
    hc = _even_attn(ctx, mods[0], True, (qa_c, qb_c), ctx_kv, None, w_out, TILE)
    hl = _even_attn(x, mods[0], False, (qa_l, qb_l), ctx_kv, (ka_l, va_l, kb_l, vb_l), w_out, TILE)
    ffn0 = _ffn_weights(ffn_up[0], ffn_conv_w[0], ffn_conv_b[0], ffn_down[0])
    g2 = norm2_g[0][None, :]
    hc = _conv_ffn(hc, mods[0], True, g2, *ffn0, step=ctx_len)
    hl = _conv_ffn(hl, mods[0], False, g2, *ffn0, step=LATENT_STEP)

    w_in = od_w_in[0]
    o2, o3 = 2 * C_WIDTH + A_Q, 2 * C_WIDTH + A_Q + A_KV
    w_row = w_in[:, o2:o3].astype(BF16)
    w_t = jnp.concatenate([w_in[:, :o2], w_in[:, o3:]], axis=1).T.astype(BF16)
    ws_t = od_sgu_w[0].transpose(0, 2, 1).astype(BF16)
    odd_wts = [w_row, w_t, _col(od_ln_g[0]), _col(od_ln_b[0]), ws_t, od_sgu_b[0][:, None, :]]
    odd_tabs = [tabs[k] for k in ("cos_ht", "sin_ht", "cos_k", "sin_k")]
    g1 = norm1_g[1][None, :]
    kd_c, vd_c = _odd_kv(hc, mods[1], g1, w_row, w_in[:, o3:].T.astype(BF16))
    yc, qd, kd_l, vd_l = _odd_proj(hl, mods[1], g1, odd_wts, odd_tabs, LATENT_STEP)
    hl = _window_attn(od_sink[0], hl, mods[1], yc, qd, (kd_c, vd_c), (kd_l, vd_l), od_w_out[0].astype(BF16))
    return _conv_ffn(hl, mods[1], False, norm2_g[1][None, :],
                     *_ffn_weights(ffn_up[1], ffn_conv_w[1], ffn_conv_b[1], ffn_down[1]),
                     step=LATENT_STEP, final_g=final_g[None, :])
```

```python
import functools

import jax
import jax.numpy as jnp
import numpy as np
from jax import lax
from jax.experimental import pallas as pl
from jax.experimental.pallas import tpu as pltpu

F32 = jnp.float32
BF16 = jnp.bfloat16

D_MODEL = 1024
HEAD_DIM = 64
GRID_W = 64
ROPE_THETA = 10000.0
NORM_EPS = 1e-6
NEG_INF = -1e30
LOG2_E = 1.4426950408889634
N_HEADS_HALF = 8
KV_HEADS = 2
GQA_GROUP = N_HEADS_HALF // KV_HEADS
A_Q = N_HEADS_HALF * HEAD_DIM
A_KV = KV_HEADS * HEAD_DIM
B_Q_RANK = 256
B_KV_RANK = 128
B_NOPE = 64
B_ROPE = 32
B_V = 64
B_QK = B_NOPE + B_ROPE
C_WIDTH = 512
C_GROUPS = 8
C_CHUNK = 128
WINDOW = 128
D_FF = 2816
LANES = 128
SUBLANES = 8
BF16_ROWS = 16
TILE = 256
LATENT_STEP = 1024
FFN_STEP = 512
FF_CHUNK = 256
FFN_BUFFERS = 2
N_SCORE_BUFFERS = 3
N_SCORE_BUFFERS_CTX = 6
VMEM_LIMIT = 56 * 1024 * 1024


def _dot(a, b):
    return jnp.dot(a, b, preferred_element_type=F32)


def _dot_nt(a, b):
    return lax.dot_general(a, b, (((1,), (1,)), ((), ())), preferred_element_type=F32)


def _dot_tn(a, b):
    return lax.dot_general(a, b, (((0,), (0,)), ((), ())), preferred_element_type=F32)


def _lane_tile(v, width):
    return jnp.concatenate([v] * (width // LANES), axis=1)


def _norm_modulate(h, g, shift, scale):
    ms = jnp.mean(h * h, axis=-1, keepdims=True)
    y = h * lax.rsqrt(ms + NORM_EPS) * g
    return y * (1.0 + scale) + shift


def _rope_rows(x, cos, sin):
    half = cos.shape[0]
    x1 = x[..., :half, :]
    x2 = x[..., half:, :]
    return jnp.concatenate([x1 * cos - x2 * sin, x1 * sin + x2 * cos], axis=-2)


def _rope_lanes(x, cos, sin_signed, half, first_half_mask):
    up = pltpu.roll(x, LANES - half, 1)
    down = pltpu.roll(x, half, 1)
    return x * cos + jnp.where(first_half_mask, up, down) * sin_signed


def _const_spec(shape):
    nd = len(shape)
    return pl.BlockSpec(shape, lambda b, t, _n=nd: (0,) * _n)


def _batch_spec(arr):
    nd = arr.ndim - 1
    return pl.BlockSpec((None,) + arr.shape[1:], lambda b, t, _n=nd: (b,) + (0,) * _n)


def _mod_spec(mod, shared_row):
    n_rows, _, d = mod.shape
    if shared_row:
        return pl.BlockSpec((None, 6, d), lambda b, t: (n_rows - 1, 0, 0))
    return pl.BlockSpec((None, 6, d), lambda b, t: (b, 0, 0))


def _params():
    return pltpu.CompilerParams(dimension_semantics=("parallel", "arbitrary"), vmem_limit_bytes=VMEM_LIMIT)


def _mod_kernel(c_ref, w_ref, b_ref, o_ref):
    c = c_ref[...]
    a = c * jax.nn.sigmoid(c)
    o_ref[...] = jnp.dot(a, w_ref[...], preferred_element_type=F32,
                         precision=lax.Precision.HIGHEST) + b_ref[...]


def _modulation(cc, mod_w, mod_b):
    depth, d, n = mod_w.shape
    rows = cc.shape[0]
    bn = 1024
    return pl.pallas_call(
        _mod_kernel,
        grid=(depth, n // bn),
        in_specs=[
            pl.BlockSpec((rows, d), lambda l, j: (0, 0)),
            pl.BlockSpec((None, d, bn), lambda l, j: (l, 0, j)),
            pl.BlockSpec((None, 1, bn), lambda l, j: (l, 0, j)),
        ],
        out_specs=pl.BlockSpec((None, rows, bn), lambda l, j: (l, 0, j)),
        out_shape=jax.ShapeDtypeStruct((depth, rows, n), F32),
        compiler_params=pltpu.CompilerParams(dimension_semantics=("arbitrary", "arbitrary")),
        name="modulation",
    )(cc, mod_w, mod_b.reshape(depth, 1, n))


def _even_proj_kernel(rotate, *refs):
    (h_ref, mod_ref, g1_ref, w_row_ref, w_t_ref, wq_up_ref, wk_up_ref, wv_up_ref,
     qa_g_ref, ka_g_ref, qlat_g_ref, kvlat_gc_ref, kvlat_gr_ref) = refs[:13]
    if rotate:
        (cos_ht_ref, sin_ht_ref, cos_rt_ref, sin_rt_ref,
         cos_k_ref, sin_k_ref, cos_kr_ref, sin_kr_ref) = refs[13:21]
    qa_ref, ka_ref, va_ref, qb_ref, kb_ref, vb_ref = refs[-6:]
    tile = TILE
    lane = lax.broadcasted_iota(jnp.int32, (tile, LANES), 1)
    head0 = lane < HEAD_DIM
    for s in range(h_ref.shape[0] // tile):
        rows = slice(s * tile, (s + 1) * tile)
        z = _norm_modulate(h_ref[rows, :], g1_ref[...], mod_ref[0:1, :], mod_ref[1:2, :]).astype(BF16)
        o = A_Q + A_KV
        ht_cq = _dot_nt(w_t_ref[o:o + B_Q_RANK, :], z)
        ht_ckv = _dot_nt(w_t_ref[o + B_Q_RANK:, :], z)
        hr = _dot(z, w_row_ref[...])
        ht_a = _dot_nt(w_t_ref[0:o, :], z)

        qa = ht_a[0:A_Q].reshape(N_HEADS_HALF, HEAD_DIM, tile)
        r = lax.rsqrt(jnp.mean(qa * qa, axis=1, keepdims=True) + NORM_EPS)
        qa = qa * r * _lane_tile(qa_g_ref[...], tile)
        if rotate:
            qa = _rope_rows(qa, cos_ht_ref[s], sin_ht_ref[s])
        qa_ref[s] = qa.reshape(A_Q, tile).astype(BF16)

        ka = hr[:, 0:A_KV]
        sq = ka * ka
        ss0 = jnp.sum(jnp.where(head0, sq, 0.0), axis=-1, keepdims=True)
        ss1 = jnp.sum(jnp.where(head0, 0.0, sq), axis=-1, keepdims=True)
        r = lax.rsqrt(jnp.where(head0, ss0, ss1) * (1.0 / HEAD_DIM) + NORM_EPS)
        ka = ka * r * ka_g_ref[...]
        if rotate:
            ka = _rope_lanes(ka, cos_k_ref[s], sin_k_ref[s], HEAD_DIM // 2,
                             (lane & (HEAD_DIM - 1)) < HEAD_DIM // 2)
        ka_ref[s] = ka.astype(BF16)

        va_ref[:, rows] = ht_a[A_Q:A_Q + A_KV].astype(BF16)

        cq = ht_cq
        r = lax.rsqrt(jnp.mean(cq * cq, axis=0, keepdims=True) + NORM_EPS)
        cq = (cq * r * _lane_tile(qlat_g_ref[...], tile)).astype(BF16)
        qb = _dot(wq_up_ref[...], cq).reshape(N_HEADS_HALF, LANES, tile)
        if rotate:
            q_rope = _rope_rows(qb[:, B_NOPE:B_QK, :], cos_rt_ref[s], sin_rt_ref[s])
            qb = jnp.concatenate([qb[:, :B_NOPE, :], q_rope, qb[:, B_QK:, :]], axis=1)
        qb = qb * (B_QK ** -0.5 * LOG2_E)
        qb_ref[s] = qb.reshape(N_HEADS_HALF * LANES, tile).astype(BF16)

        ckv = hr[:, A_KV:A_KV + B_KV_RANK]
        r = lax.rsqrt(jnp.mean(ckv * ckv, axis=-1, keepdims=True) + NORM_EPS)
        ckv = (ckv * r * kvlat_gr_ref[...]).astype(BF16)
        k_nope = _dot(ckv, wk_up_ref[...])
        kr = hr[:, A_KV + B_KV_RANK:A_KV + B_KV_RANK + LANES]
        if rotate:
            kr = _rope_lanes(kr, cos_kr_ref[s], sin_kr_ref[s], B_ROPE // 2, lane < B_NOPE + B_ROPE // 2)
        for hd in range(N_HEADS_HALF):
            kb_ref[s, hd] = (k_nope[:, hd * LANES:(hd + 1) * LANES] + kr).astype(BF16)

        ckvt = ht_ckv
        r = lax.rsqrt(jnp.mean(ckvt * ckvt, axis=0, keepdims=True) + NORM_EPS)
        ckvt = (ckvt * r * _lane_tile(kvlat_gc_ref[...], tile)).astype(BF16)
        vb_ref[:, rows] = _dot(wv_up_ref[...], ckvt).astype(BF16)


def _even_proj(h, mod, shared_mod, g1, wts, tabs, step):
    bsz, seq, d = h.shape
    n_sub = step // TILE
    n_tiles = seq // TILE
    rotate = tabs is not None

    def tiles_spec(*dims):
        nd = len(dims)
        return pl.BlockSpec((None, n_sub) + dims, lambda b, t, _n=nd: (b, t) + (0,) * _n)

    def seq_spec(rows):
        return pl.BlockSpec((None, rows, step), lambda b, t: (b, 0, t))

    in_specs = [pl.BlockSpec((None, step, d), lambda b, t: (b, t, 0)), _mod_spec(mod, shared_mod),
                _const_spec(g1.shape)]
    in_specs += [_const_spec(w.shape) for w in wts]
    if rotate:
        in_specs += [pl.BlockSpec((n_sub,) + tb.shape[1:], lambda b, t: (t, 0, 0)) for tb in tabs]
    out_specs = [
        tiles_spec(A_Q, TILE), tiles_spec(TILE, A_KV), seq_spec(A_KV),
        tiles_spec(N_HEADS_HALF * LANES, TILE), tiles_spec(N_HEADS_HALF, TILE, LANES),
        seq_spec(N_HEADS_HALF * B_V),
    ]
    out_shapes = [
        (bsz, n_tiles, A_Q, TILE), (bsz, n_tiles, TILE, A_KV), (bsz, A_KV, seq),
        (bsz, n_tiles, N_HEADS_HALF * LANES, TILE), (bsz, n_tiles, N_HEADS_HALF, TILE, LANES),
        (bsz, N_HEADS_HALF * B_V, seq),
    ]
    return pl.pallas_call(
        functools.partial(_even_proj_kernel, rotate),
        grid=(bsz, seq // step),
        in_specs=in_specs,
        out_specs=out_specs,
        out_shape=[jax.ShapeDtypeStruct(shp, BF16) for shp in out_shapes],
        compiler_params=_params(),
        name="even_proj",
    )(h, mod, g1, *wts, *(tabs if rotate else ()))


def _scores(key_parts, q, s_ref):
    tile = q.shape[1]
    off = 0
    m = None
    for keys in key_parts:
        n = keys.shape[0]
        s = _dot(keys, q)
        s_ref[off:off + n, :] = s
        part = jnp.max(s.reshape(n // SUBLANES, SUBLANES, tile), axis=0)
        m = part if m is None else jnp.maximum(m, part)
        off += n
    return jnp.max(m, axis=0, keepdims=True)


def _weighted_values(s_ref, n_keys, m, value_chunk, dv):
    tile = s_ref.shape[1]
    groups = TILE // SUBLANES
    l = jnp.zeros((SUBLANES, tile), F32)
    acc = jnp.zeros((dv, tile), F32)
    for c in range(n_keys // TILE):
        p = jnp.exp2(s_ref[c * TILE:(c + 1) * TILE, :] - m)
        l = l + jnp.sum(p.reshape(groups, SUBLANES, tile), axis=0)
        acc = acc + _dot(value_chunk(c), p.astype(BF16))
    l = jnp.sum(l, axis=0, keepdims=True)
    return acc * (1.0 / l)


def _even_attn_kernel(with_latent_keys, n_buffers, *refs):
    h_ref, mod_ref, qa_ref, qb_ref, kac_ref, vac_ref, kbc_ref, vbc_ref = refs[:8]
    if with_latent_keys:
        kal_ref, val_ref, kbl_ref, vbl_ref = refs[8:12]
    w_out_ref, o_ref = refs[-3 - n_buffers:-1 - n_buffers]
    s_refs = refs[-1 - n_buffers:-1]
    y_ref = refs[-1]
    n_ctx = kac_ref.shape[0] * TILE
    n_keys = n_ctx + (kal_ref.shape[0] * TILE if with_latent_keys else 0)

    def rows_of(index, size):
        return slice(index * size, (index + 1) * size)

    def value_chunk(c, rows, ctx_ref, lat_ref):
        if c * TILE < n_ctx:
            return ctx_ref[rows, c * TILE:(c + 1) * TILE]
        return lat_ref[rows, c * TILE - n_ctx:(c + 1) * TILE - n_ctx]

    def a_scores(qt, hd, s_ref):
        q = qa_ref[qt, rows_of(hd, HEAD_DIM), :]
        zeros = jnp.zeros_like(q)
        q = jnp.concatenate([q, zeros] if hd < GQA_GROUP else [zeros, q], axis=0)
        parts = [kac_ref[...].reshape(n_ctx, A_KV)]
        if with_latent_keys:
            parts.append(kal_ref[...].reshape(n_keys - n_ctx, A_KV))
        return _scores(parts, q, s_ref)

    def a_output(qt, hd, s_ref, m):
        v_rows = rows_of(hd // GQA_GROUP, HEAD_DIM)
        out = _weighted_values(s_ref, n_keys, m,
                               lambda c: value_chunk(c, v_rows, vac_ref, val_ref if with_latent_keys else None),
                               HEAD_DIM)
        y_ref[qt, rows_of(hd, HEAD_DIM), :] = out.astype(BF16)

    def b_scores(qt, hd, s_ref):
        parts = [kbc_ref[:, hd].reshape(n_ctx, LANES)]
        if with_latent_keys:
            parts.append(kbl_ref[:, hd].reshape(n_keys - n_ctx, LANES))
        return _scores(parts, qb_ref[qt, rows_of(hd, LANES), :], s_ref)

    def b_output(qt, hd, s_ref, m):
        v_rows = rows_of(hd, B_V)
        out = _weighted_values(s_ref, n_keys, m,
                               lambda c: value_chunk(c, v_rows, vbc_ref, vbl_ref if with_latent_keys else None),
                               B_V)
        y_ref[qt, rows_of(N_HEADS_HALF + hd, B_V), :] = out.astype(BF16)

    heads = []
    for qt in range(qa_ref.shape[0]):
        heads += [(a_scores, a_output, qt, hd) for hd in range(N_HEADS_HALF)]
        heads += [(b_scores, b_output, qt, hd) for hd in range(N_HEADS_HALF)]
    lookahead = len(s_refs) - 1
    maxes = {i: heads[i][0](heads[i][2], heads[i][3], s_refs[i % len(s_refs)]) for i in range(lookahead)}
    for i, (_, output, qt, hd) in enumerate(heads):
        j = i + lookahead
        if j < len(heads):
            maxes[j] = heads[j][0](heads[j][2], heads[j][3], s_refs[j % len(s_refs)])
        output(qt, hd, s_refs[i % len(s_refs)], maxes.pop(i))
        if i + 1 == len(heads) or heads[i + 1][2] != qt:
            rows = rows_of(qt, TILE)
            o_ref[rows, :] = h_ref[rows, :] + mod_ref[2:3, :] * _dot_tn(y_ref[qt], w_out_ref[...])


def _even_attn(h, mod, shared_mod, q, ctx_kv, lat_kv, w_out, step):
    bsz, seq, d = h.shape
    n_q = step // TILE
    qa, qb = q
    with_latent_keys = lat_kv is not None
    n_buffers = N_SCORE_BUFFERS if with_latent_keys else N_SCORE_BUFFERS_CTX
    kv = list(ctx_kv) + (list(lat_kv) if with_latent_keys else [])
    n_keys = sum(arr.shape[1] for arr in (ctx_kv[0],) + ((lat_kv[0],) if with_latent_keys else ())) * TILE

    def tile_spec(arr):
        return pl.BlockSpec((None, n_q) + arr.shape[2:], lambda b, t: (b, t, 0, 0))

    return pl.pallas_call(
        functools.partial(_even_attn_kernel, with_latent_keys, n_buffers),
        grid=(bsz, seq // step),
        in_specs=[pl.BlockSpec((None, step, d), lambda b, t: (b, t, 0)), _mod_spec(mod, shared_mod),
                  tile_spec(qa), tile_spec(qb)] + [_batch_spec(arr) for arr in kv] + [_const_spec(w_out.shape)],
        out_specs=pl.BlockSpec((None, step, d), lambda b, t: (b, t, 0)),
        out_shape=jax.ShapeDtypeStruct((bsz, seq, d), F32),
        scratch_shapes=[pltpu.VMEM((n_keys, TILE), F32)] * n_buffers
        + [pltpu.VMEM((n_q, D_MODEL, TILE), BF16)],
        compiler_params=_params(),
        name="even_attn",
    )(h, mod, qa, qb, *kv, w_out)


def _conv_ffn_kernel(final_norm, *refs):
    h_ref, prev_ref, next_ref, mod_ref, g2_ref, w_up_ref, cw_ref, cb_ref, w_down_ref = refs[:9]
    if final_norm:
        gf_ref = refs[9]
    o_ref = refs[-2 - 2 * FFN_BUFFERS]
    hs_flat = refs[-1 - 2 * FFN_BUFFERS:-1]
    act_ref = refs[-1]
    t = pl.program_id(1)
    last_tile = pl.num_programs(1) - 1
    tile = h_ref.shape[0]
    n_chunks = D_FF // FF_CHUNK
    g2 = g2_ref[...]
    shift, scale = mod_ref[3:4, :], mod_ref[4:5, :]
    has_prev = (t != 0).astype(F32)
    has_next = (t != last_tile).astype(F32)
    h = h_ref[...]
    z = jnp.concatenate([
        _norm_modulate(h, g2, shift, scale),
        _norm_modulate(prev_ref[...], g2, shift, scale) * has_prev,
        _norm_modulate(next_ref[...], g2, shift, scale) * has_next,
    ], axis=0).astype(BF16)

    hs_refs = [hs_flat[2 * i:2 * i + 2] for i in range(FFN_BUFFERS)]
    groups = tile // SUBLANES
    sub = lax.broadcasted_iota(jnp.int32, (groups, SUBLANES, FF_CHUNK), 1)

    def up(j):
        for hs_ref, col in zip(hs_refs[j % FFN_BUFFERS], (j * FF_CHUNK, D_FF + j * FF_CHUNK)):
            hs_ref[...] = _dot(z, w_up_ref[:, col:col + FF_CHUNK])

    def conv(hs_ref, col):
        hs = hs_ref[0:tile, :].reshape(groups, SUBLANES, FF_CHUNK)
        before = hs_ref[tile:tile + SUBLANES, :].reshape(1, SUBLANES, FF_CHUNK)
        after = hs_ref[tile + SUBLANES:tile + 2 * SUBLANES, :].reshape(1, SUBLANES, FF_CHUNK)
        down = pltpu.roll(jnp.concatenate([before, hs], axis=0), 1, 1)
        up_ = pltpu.roll(jnp.concatenate([hs, after], axis=0), SUBLANES - 1, 1)
        below = jnp.where(sub == 0, down[:groups], down[1:])
        above = jnp.where(sub == SUBLANES - 1, up_[1:], up_[:groups])
        cw = cw_ref[:, col:col + FF_CHUNK]
        return below * cw[0:1, :] + hs * cw[1:2, :] + above * cw[2:3, :] + cb_ref[:, col:col + FF_CHUNK]

    def gate(j):
        a = conv(hs_refs[j % FFN_BUFFERS][0], j * FF_CHUNK)
        g = conv(hs_refs[j % FFN_BUFFERS][1], D_FF + j * FF_CHUNK)
        act = (g * jax.nn.sigmoid(g) * a).reshape(tile, FF_CHUNK)
        act_ref[:, j * FF_CHUNK:(j + 1) * FF_CHUNK] = act.astype(BF16)

    lookahead = FFN_BUFFERS - 1
    for j in range(lookahead):
        up(j)
    for j in range(n_chunks):
        if j + lookahead < n_chunks:
            up(j + lookahead)
        gate(j)
    out = h + mod_ref[5:6, :] * _dot(act_ref[...], w_down_ref[...])
    if final_norm:
        ms = jnp.mean(out * out, axis=-1, keepdims=True)
        out = out * lax.rsqrt(ms + NORM_EPS) * gf_ref[...]
    o_ref[...] = out


def _conv_ffn(h, mod, shared_mod, g2, w_up, conv_w, conv_b, w_down, step, final_g=None):
    bsz, seq, d = h.shape
    rows_per_step = step // SUBLANES
    last_block = seq // SUBLANES - 1
    final_norm = final_g is not None

    def weight_spec(w):
        return pl.BlockSpec(w.shape, lambda b, t, _n=w.ndim: (0,) * _n, pipeline_mode=pl.Buffered(1))

    in_specs = [
        pl.BlockSpec((None, step, d), lambda b, t: (b, t, 0)),
        pl.BlockSpec((None, SUBLANES, d), lambda b, t: (b, jnp.maximum(t * rows_per_step - 1, 0), 0)),
        pl.BlockSpec((None, SUBLANES, d),
                     lambda b, t: (b, jnp.minimum((t + 1) * rows_per_step, last_block), 0)),
        _mod_spec(mod, shared_mod),
        _const_spec(g2.shape), weight_spec(w_up), _const_spec(conv_w.shape),
        _const_spec(conv_b.shape), weight_spec(w_down),
    ]
    args = [h, h, h, mod, g2, w_up, conv_w, conv_b, w_down]
    if final_norm:
        in_specs.append(_const_spec(final_g.shape))
        args.append(final_g)
    return pl.pallas_call(
        functools.partial(_conv_ffn_kernel, final_norm),
        grid=(bsz, seq // step),
        in_specs=in_specs,
        out_specs=pl.BlockSpec((None, step, d), lambda b, t: (b, t, 0)),
        out_shape=jax.ShapeDtypeStruct((bsz, seq, d), F32),
        scratch_shapes=[pltpu.VMEM((step + 2 * SUBLANES, FF_CHUNK), F32)] * (2 * FFN_BUFFERS)
        + [pltpu.VMEM((step, D_FF), BF16)],
        compiler_params=_params(),
        name="conv_ffn",
    )(*args)


def _gelu(x):
    return 0.5 * x * (1.0 + jnp.tanh(0.7978845608028654 * (x + 0.044715 * (x * x * x))))


def _odd_proj_kernel(h_ref, mod_ref, g1_ref, w_row_ref, w_t_ref, ln_g_ref, ln_b_ref, ws_ref, bs_ref,
                     cos_ht_ref, sin_ht_ref, cos_k_ref, sin_k_ref,
                     yc_ref, qd_ref, kd_ref, vd_ref):
    tile = TILE
    chunks = tile // C_CHUNK
    lane = lax.broadcasted_iota(jnp.int32, (tile, LANES), 1)
    gw = C_WIDTH // C_GROUPS
    for s in range(h_ref.shape[0] // tile):
        z = _norm_modulate(h_ref[s * tile:(s + 1) * tile, :], g1_ref[...],
                           mod_ref[0:1, :], mod_ref[1:2, :]).astype(BF16)
        ht_v = _dot_nt(w_t_ref[C_WIDTH:2 * C_WIDTH, :], z)
        ht_u = _dot_nt(w_t_ref[0:C_WIDTH, :], z)
        ht_d = _dot_nt(w_t_ref[2 * C_WIDTH:, :], z)
        hr = _dot(z, w_row_ref[...])

        v = _gelu(ht_v)
        mu = jnp.mean(v, axis=0, keepdims=True)
        vc = v - mu
        var = jnp.mean(vc * vc, axis=0, keepdims=True)
        vn = vc * lax.rsqrt(var + NORM_EPS) * _lane_tile(ln_g_ref[...], tile) + _lane_tile(ln_b_ref[...], tile)
        vn = vn.astype(BF16)
        for g in range(C_GROUPS):
            for n in range(chunks):
                cols = slice(n * C_CHUNK, (n + 1) * C_CHUNK)
                rows = slice(g * gw, (g + 1) * gw)
                mixed = _dot(vn[rows, cols], ws_ref[g]) + bs_ref[g]
                yc_ref[s, rows, cols] = (_gelu(ht_u[rows, cols]) * mixed).astype(BF16)

        qd = ht_d[0:A_Q].reshape(N_HEADS_HALF, HEAD_DIM, tile)
        qd = _rope_rows(qd, cos_ht_ref[s], sin_ht_ref[s]) * (HEAD_DIM ** -0.5 * LOG2_E)
        qd_ref[s] = qd.reshape(A_Q, tile).astype(BF16)
        kd = _rope_lanes(hr, cos_k_ref[s], sin_k_ref[s], HEAD_DIM // 2,
                         (lane & (HEAD_DIM - 1)) < HEAD_DIM // 2).astype(BF16)
        vd = ht_d[A_Q:A_Q + A_KV].astype(BF16)
        for n in range(chunks):
            kd_ref[s * chunks + n] = kd[n * C_CHUNK:(n + 1) * C_CHUNK, :]
            vd_ref[s * chunks + n] = vd[:, n * C_CHUNK:(n + 1) * C_CHUNK]


def _odd_proj(h, mod, g1, wts, tabs, step):
    bsz, seq, d = h.shape
    n_sub = step // TILE
    n_tiles = seq // TILE
    chunks = TILE // C_CHUNK

    def tiles_spec(rows):
        return pl.BlockSpec((None, n_sub, rows, TILE), lambda b, t: (b, t, 0, 0))

    chunk_spec = pl.BlockSpec((None, n_sub * chunks, C_CHUNK, C_CHUNK), lambda b, t: (b, t, 0, 0))
    in_specs = [pl.BlockSpec((None, step, d), lambda b, t: (b, t, 0)), _mod_spec(mod, False),
                _const_spec(g1.shape)]
    in_specs += [_const_spec(w.shape) for w in wts]
    in_specs += [pl.BlockSpec((n_sub,) + tb.shape[1:], lambda b, t: (t, 0, 0)) for tb in tabs]
    return pl.pallas_call(
        _odd_proj_kernel,
        grid=(bsz, seq // step),
        in_specs=in_specs,
        out_specs=[tiles_spec(C_WIDTH), tiles_spec(A_Q), chunk_spec, chunk_spec],
        out_shape=[jax.ShapeDtypeStruct((bsz, n_tiles, C_WIDTH, TILE), BF16),
                   jax.ShapeDtypeStruct((bsz, n_tiles, A_Q, TILE), BF16),
                   jax.ShapeDtypeStruct((bsz, n_tiles * chunks, C_CHUNK, A_KV), BF16),
                   jax.ShapeDtypeStruct((bsz, n_tiles * chunks, A_KV, C_CHUNK), BF16)],
        compiler_params=_params(),
        name="odd_proj",
    )(h, mod, g1, *wts, *tabs)


def _odd_kv_kernel(h_ref, mod_ref, g1_ref, w_k_ref, w_vt_ref, kd_ref, vd_ref):
    z = _norm_modulate(h_ref[...], g1_ref[...], mod_ref[0:1, :], mod_ref[1:2, :]).astype(BF16)
    kd = _dot(z, w_k_ref[...]).astype(BF16)
    vd = _dot_nt(w_vt_ref[...], z).astype(BF16)
    for n in range(h_ref.shape[0] // C_CHUNK):
        kd_ref[n] = kd[n * C_CHUNK:(n + 1) * C_CHUNK, :]
        vd_ref[n] = vd[:, n * C_CHUNK:(n + 1) * C_CHUNK]


def _odd_kv(h, mod, g1, w_k, w_vt):
    bsz, seq, d = h.shape
    chunks = seq // C_CHUNK
    chunk_spec = pl.BlockSpec((None, chunks, C_CHUNK, C_CHUNK), lambda b, t: (b, 0, 0, 0))
    return pl.pallas_call(
        _odd_kv_kernel,
        grid=(bsz, 1),
        in_specs=[pl.BlockSpec((None, seq, d), lambda b, t: (b, 0, 0)), _mod_spec(mod, True),
                  _const_spec(g1.shape), _const_spec(w_k.shape), _const_spec(w_vt.shape)],
        out_specs=[chunk_spec, chunk_spec],
        out_shape=[jax.ShapeDtypeStruct((bsz, chunks, C_CHUNK, A_KV), BF16),
                   jax.ShapeDtypeStruct((bsz, chunks, A_KV, C_CHUNK), BF16)],
        compiler_params=_params(),
        name="odd_kv",
    )(h, mod, g1, w_k, w_vt)


def _window_attn_kernel(sink_ref, h_ref, mod_ref, yc_ref, qd_ref, kdc_ref, vdc_ref, kdl_ref, vdl_ref, w_out_ref,
                        o_ref, s0_ref, s1_ref, s2_ref, s3_ref, s4_ref, s5_ref, bias_ref, y_ref):
    t = pl.program_id(1)
    tile = qd_ref.shape[1]
    ctx_chunks = kdc_ref.shape[0]
    lat_chunks = kdl_ref.shape[0]
    sub = tile // C_CHUNK
    win_chunks = sub + 2
    w0 = jnp.clip(t * sub - 1, 0, lat_chunks - win_chunks)
    n_ctx = ctx_chunks * C_CHUNK
    n_win = win_chunks * C_CHUNK
    pair = 2 * C_CHUNK
    s_refs = (s0_ref, s1_ref, s2_ref, s3_ref, s4_ref, s5_ref)

    k_ctx = kdc_ref[...].reshape(n_ctx, A_KV)
    k_win = kdl_ref[pl.ds(w0, win_chunks)].reshape(n_win, A_KV)
    keys = jnp.concatenate([k_ctx, k_win], axis=0)
    key_pos = w0 * C_CHUNK + lax.broadcasted_iota(jnp.int32, (n_win, tile), 0)
    qry_pos = t * tile + lax.broadcasted_iota(jnp.int32, (n_win, tile), 1)
    bias_ref[...] = jnp.where(jnp.abs(qry_pos - key_pos) <= WINDOW, 0.0, NEG_INF)
    zeros = jnp.zeros((HEAD_DIM, tile), BF16)
    ones = jnp.ones((BF16_ROWS, pair), BF16)

    def scores(hd):
        s_ref = s_refs[hd % len(s_refs)]
        q = qd_ref[hd * HEAD_DIM:(hd + 1) * HEAD_DIM, :]
        q = jnp.concatenate([q, zeros] if hd < GQA_GROUP else [zeros, q], axis=0)
        s = _dot(keys, q)
        s_ctx = s[:n_ctx]
        s_win = s[n_ctx:] + bias_ref[...]
        s_ref[0:n_ctx, :] = s_ctx
        s_ref[n_ctx:, :] = s_win
        m = jnp.maximum(jnp.max(s_ctx.reshape(n_ctx // SUBLANES, SUBLANES, tile), axis=0),
                        jnp.max(s_win.reshape(n_win // SUBLANES, SUBLANES, tile), axis=0))
        return jnp.maximum(jnp.max(m, axis=0, keepdims=True), sink_ref[hd] * LOG2_E)

    def output(hd, m):
        s_ref = s_refs[hd % len(s_refs)]
        rows = slice((hd // GQA_GROUP) * HEAD_DIM, (hd // GQA_GROUP + 1) * HEAD_DIM)
        acc = jnp.zeros((HEAD_DIM + BF16_ROWS, tile), F32)
        for c in range((n_ctx + n_win) // pair):
            p = jnp.exp2(s_ref[c * pair:(c + 1) * pair, :] - m)
            if 2 * c < ctx_chunks:
                v = [vdc_ref[2 * c, rows, :], vdc_ref[2 * c + 1, rows, :]]
            else:
                first = w0 + (2 * c - ctx_chunks)
                v = [vdl_ref[first, rows, :], vdl_ref[first + 1, rows, :]]
            v = jnp.concatenate([jnp.concatenate(v, axis=1), ones], axis=0)
            acc = acc + _dot(v, p.astype(BF16))
        l = acc[HEAD_DIM:HEAD_DIM + 1] + jnp.exp2(sink_ref[hd] * LOG2_E - m)
        y_ref[hd * HEAD_DIM:(hd + 1) * HEAD_DIM, :] = (acc[:HEAD_DIM] * (1.0 / l)).astype(BF16)

    lookahead = len(s_refs) - 1
    maxes = {hd: scores(hd) for hd in range(lookahead)}
    for hd in range(N_HEADS_HALF):
        if hd + lookahead < N_HEADS_HALF:
            maxes[hd + lookahead] = scores(hd + lookahead)
        output(hd, maxes.pop(hd))

    y = _dot_tn(yc_ref[...], w_out_ref[0:C_WIDTH, :]) + _dot_tn(y_ref[...], w_out_ref[C_WIDTH:, :])
    o_ref[...] = h_ref[...] + mod_ref[2:3, :] * y


def _window_attn(sink, h, mod, yc, qd, ctx_kv, lat_kv, w_out):
    bsz, seq, d = h.shape

    def tile_spec(rows):
        return pl.BlockSpec((None, None, rows, TILE), lambda b, t: (b, t, 0, 0))

    kv = list(ctx_kv) + list(lat_kv)
    return pl.pallas_call(
        _window_attn_kernel,
        grid=(bsz, seq // TILE),
        in_specs=[pl.BlockSpec(memory_space=pltpu.SMEM),
                  pl.BlockSpec((None, TILE, d), lambda b, t: (b, t, 0)), _mod_spec(mod, False),
                  tile_spec(C_WIDTH), tile_spec(A_Q)] + [_batch_spec(arr) for arr in kv]
        + [_const_spec(w_out.shape)],
        out_specs=pl.BlockSpec((None, TILE, d), lambda b, t: (b, t, 0)),
        out_shape=jax.ShapeDtypeStruct((bsz, seq, d), F32),
        scratch_shapes=[pltpu.VMEM((2 * TILE + 2 * C_CHUNK, TILE), F32)] * 6
        + [pltpu.VMEM((TILE + 2 * C_CHUNK, TILE), F32), pltpu.VMEM((A_Q, TILE), BF16)],
        compiler_params=_params(),
        name="window_attn",
    )(sink, h, mod, yc, qd, *kv, w_out)


def _rope_tables(seq):
    def base(dim):
        n_rows = seq // GRID_W
        rows = np.repeat(np.arange(n_rows), GRID_W).astype(np.float64)
        cols = np.tile(np.arange(GRID_W), n_rows).astype(np.float64)
        quarter = dim // 4
        inv_freq = ROPE_THETA ** (-np.arange(quarter, dtype=np.float64) / quarter)
        ang = np.concatenate([rows[:, None] * inv_freq, cols[:, None] * inv_freq], axis=-1)
        return np.cos(ang), np.sin(ang)

    n_tiles = seq // TILE

    def feature_major(tb):
        return jnp.asarray(tb.reshape(n_tiles, TILE, tb.shape[1]).transpose(0, 2, 1), F32)

    def token_major(tb):
        return jnp.asarray(tb.reshape(n_tiles, TILE, LANES), F32)

    cos_h, sin_h = base(HEAD_DIM)
    cos_r, sin_r = base(B_ROPE)
    cos_k = np.tile(cos_h, (1, LANES // (HEAD_DIM // 2)))
    sin_k = np.tile(np.concatenate([-sin_h, sin_h], axis=1), (1, LANES // HEAD_DIM))
    pad_l = np.zeros((seq, B_NOPE))
    pad_r = np.zeros((seq, LANES - B_QK))
    cos_kr = np.concatenate([pad_l, cos_r, cos_r, pad_r], axis=1)
    sin_kr = np.concatenate([pad_l, -sin_r, sin_r, pad_r], axis=1)
    return dict(cos_ht=feature_major(cos_h), sin_ht=feature_major(sin_h),
                cos_rt=feature_major(cos_r), sin_rt=feature_major(sin_r),
                cos_k=token_major(cos_k), sin_k=token_major(sin_k),
                cos_kr=token_major(cos_kr), sin_kr=token_major(sin_kr))


def _col(v):
    return jnp.broadcast_to(v.astype(F32)[:, None], (v.shape[0], LANES))


def _ffn_weights(w_up, conv_w, conv_b, w_down):
    return w_up.astype(BF16), conv_w, conv_b[None, :], w_down.astype(BF16)


def kernel(x, c, ctx, c_ctx, mod_w, mod_b, norm1_g, norm2_g, ev_w_in, ev_qa_g, ev_ka_g, ev_qlat_g, ev_w_q_up, ev_kvlat_g, ev_w_kv_up, ev_w_out, od_w_in, od_ln_g, od_ln_b, od_sgu_w, od_sgu_b, od_sink, od_w_out, ffn_up, ffn_conv_w, ffn_conv_b, ffn_down, final_g):
    bsz, seq, d = x.shape
    ctx_len = ctx.shape[1]
    assert d == D_MODEL and ctx_len % TILE == 0 and seq % FFN_STEP == 0 and mod_w.shape[0] == 2

    n_rows = -(-(bsz + 1) // SUBLANES) * SUBLANES
    cc = jnp.concatenate([c, jnp.zeros((n_rows - bsz - 1, d), F32), c_ctx[None, :]], axis=0)
    mods = _modulation(cc, mod_w, mod_b).reshape(2, n_rows, 6, d)
    tabs = _rope_tables(seq)

    w_in = ev_w_in[0]
    s0, s1, s2, s3, s4 = A_Q, A_Q + A_KV, A_Q + 2 * A_KV, A_Q + 2 * A_KV + B_Q_RANK, A_Q + 2 * A_KV + B_Q_RANK + B_KV_RANK
    w_kr = jnp.zeros((d, LANES), F32).at[:, B_NOPE:B_QK].set(w_in[:, s4:])
    w_row = jnp.concatenate([w_in[:, s0:s1], w_in[:, s3:s4], w_kr], axis=1).astype(BF16)
    w_t = jnp.concatenate([w_in[:, :s0], w_in[:, s1:s2], w_in[:, s2:s3], w_in[:, s3:s4]], axis=1).T.astype(BF16)
    wq = ev_w_q_up[0].reshape(B_Q_RANK, N_HEADS_HALF, B_QK)
    wq = jnp.pad(wq, ((0, 0), (0, 0), (0, LANES - B_QK))).reshape(B_Q_RANK, N_HEADS_HALF * LANES).T.astype(BF16)
    wkv = ev_w_kv_up[0].reshape(B_KV_RANK, N_HEADS_HALF, B_NOPE + B_V)
    wk = jnp.pad(wkv[:, :, :B_NOPE], ((0, 0), (0, 0), (0, LANES - B_NOPE)))
    wk = wk.reshape(B_KV_RANK, N_HEADS_HALF * LANES).astype(BF16)
    wv = wkv[:, :, B_NOPE:].reshape(B_KV_RANK, N_HEADS_HALF * B_V).T.astype(BF16)
    even_wts = [w_row, w_t, wq, wk, wv,
                _col(ev_qa_g[0] * (HEAD_DIM ** -0.5 * LOG2_E)), jnp.tile(ev_ka_g[0], KV_HEADS)[None, :],
                _col(ev_qlat_g[0]), _col(ev_kvlat_g[0]), ev_kvlat_g[0][None, :]]
    even_tabs = [tabs[k] for k in ("cos_ht", "sin_ht", "cos_rt", "sin_rt", "cos_k", "sin_k", "cos_kr", "sin_kr")]
    g1 = norm1_g[0][None, :]
    qa_c, ka_c, va_c, qb_c, kb_c, vb_c = _even_proj(ctx, mods[0], True, g1, even_wts, None, ctx_len)
    qa_l, ka_l, va_l, qb_l, kb_l, vb_l = _even_proj(x, mods[0], False, g1, even_wts, even_tabs, LATENT_STEP)
    ctx_kv = (ka_c, va_c, kb_c, vb_c)
    w_out = ev_w_out[0].astype(BF16)
    hc = _even_attn(ctx, mods[0], True, (qa_c, qb_c), ctx_kv, None, w_out, TILE)
    hl = _even_attn(x, mods[0], False, (qa_l, qb_l), ctx_kv, (ka_l, va_l, kb_l, vb_l), w_out, TILE)
    ffn0 = _ffn_weights(ffn_up[0], ffn_conv_w[0], ffn_conv_b[0], ffn_down[0])
    g2 = norm2_g[0][None, :]
    hc = _conv_ffn(hc, mods[0], True, g2, *ffn0, step=ctx_len)
    hl = _conv_ffn(hl, mods[0], False, g2, *ffn0, step=FFN_STEP)

    w_in = od_w_in[0]
    o2, o3 = 2 * C_WIDTH + A_Q, 2 * C_WIDTH + A_Q + A_KV
    w_row = w_in[:, o2:o3].astype(BF16)
    w_t = jnp.concatenate([w_in[:, :o2], w_in[:, o3:]], axis=1).T.astype(BF16)
    ws_t = od_sgu_w[0].transpose(0, 2, 1).astype(BF16)
    odd_wts = [w_row, w_t, _col(od_ln_g[0]), _col(od_ln_b[0]), ws_t, od_sgu_b[0][:, None, :]]
    odd_tabs = [tabs[k] for k in ("cos_ht", "sin_ht", "cos_k", "sin_k")]
    g1 = norm1_g[1][None, :]
    kd_c, vd_c = _odd_kv(hc, mods[1], g1, w_row, w_in[:, o3:].T.astype(BF16))
    yc, qd, kd_l, vd_l = _odd_proj(hl, mods[1], g1, odd_wts, odd_tabs, LATENT_STEP)
    hl = _window_attn(od_sink[0], hl, mods[1], yc, qd, (kd_c, vd_c), (kd_l, vd_l), od_w_out[0].astype(BF16))
    return _conv_ffn(hl, mods[1], False, norm2_g[1][None, :],
                     *_ffn_weights(ffn_up[1], ffn_conv_w[1], ffn_conv_b[1], ffn_down[1]),
                     step=FFN_STEP, final_g=final_g[None, :])
```

```python
import functools

import jax
import jax.numpy as jnp
import numpy as np
from jax import lax
from jax.experimental import pallas as pl
from jax.experimental.pallas import tpu as pltpu

F32 = jnp.float32
BF16 = jnp.bfloat16

D_MODEL = 1024
HEAD_DIM = 64
GRID_W = 64
ROPE_THETA = 10000.0
NORM_EPS = 1e-6
NEG_INF = -1e30
LOG2_E = 1.4426950408889634
N_HEADS_HALF = 8
KV_HEADS = 2
GQA_GROUP = N_HEADS_HALF // KV_HEADS
A_Q = N_HEADS_HALF * HEAD_DIM
A_KV = KV_HEADS * HEAD_DIM
B_Q_RANK = 256
B_KV_RANK = 128
B_NOPE = 64
B_ROPE = 32
B_V = 64
B_QK = B_NOPE + B_ROPE
C_WIDTH = 512
C_GROUPS = 8
C_CHUNK = 128
WINDOW = 128
D_FF = 2816
LANES = 128
SUBLANES = 8
BF16_ROWS = 16
TILE = 256
LATENT_STEP = 1024
FFN_STEP = 512
FF_CHUNK = 256
FFN_BUFFERS = 2
N_SCORE_BUFFERS = 3
N_SCORE_BUFFERS_CTX = 6
N_WINDOW_BUFFERS = 6
MOD_COLS = 1024
V7X_VMEM_BYTES = 64 * 1024 * 1024
VMEM_LIMIT = V7X_VMEM_BYTES - 8 * 1024 * 1024


def _dot(a, b):
    return jnp.dot(a, b, preferred_element_type=F32)


def _dot_nt(a, b):
    return lax.dot_general(a, b, (((1,), (1,)), ((), ())), preferred_element_type=F32)


def _dot_tn(a, b):
    return lax.dot_general(a, b, (((0,), (0,)), ((), ())), preferred_element_type=F32)


def _lane_tile(v, width):
    return jnp.concatenate([v] * (width // LANES), axis=1)


def _norm_modulate(h, g, shift, scale):
    ms = jnp.mean(h * h, axis=-1, keepdims=True)
    y = h * lax.rsqrt(ms + NORM_EPS) * g
    return y * (1.0 + scale) + shift


def _rope_rows(x, cos, sin):
    half = cos.shape[0]
    x1 = x[..., :half, :]
    x2 = x[..., half:, :]
    return jnp.concatenate([x1 * cos - x2 * sin, x1 * sin + x2 * cos], axis=-2)


def _rope_lanes(x, cos, sin_signed, half, first_half_mask):
    up = pltpu.roll(x, LANES - half, 1)
    down = pltpu.roll(x, half, 1)
    return x * cos + jnp.where(first_half_mask, up, down) * sin_signed


def _const_spec(shape):
    nd = len(shape)
    return pl.BlockSpec(shape, lambda b, t, _n=nd: (0,) * _n)


def _batch_spec(arr):
    nd = arr.ndim - 1
    return pl.BlockSpec((None,) + arr.shape[1:], lambda b, t, _n=nd: (b,) + (0,) * _n)


def _mod_spec(mod, shared_row):
    n_rows, _, d = mod.shape
    if shared_row:
        return pl.BlockSpec((None, 6, d), lambda b, t: (n_rows - 1, 0, 0))
    return pl.BlockSpec((None, 6, d), lambda b, t: (b, 0, 0))


def _params():
    return pltpu.CompilerParams(dimension_semantics=("parallel", "arbitrary"), vmem_limit_bytes=VMEM_LIMIT)


def _mod_kernel(c_ref, w_ref, b_ref, o_ref):
    c = c_ref[...]
    a = c * jax.nn.sigmoid(c)
    o_ref[...] = jnp.dot(a, w_ref[...], preferred_element_type=F32,
                         precision=lax.Precision.HIGHEST) + b_ref[...]


def _modulation(cc, mod_w, mod_b):
    depth, d, n = mod_w.shape
    rows = cc.shape[0]
    bn = MOD_COLS
    return pl.pallas_call(
        _mod_kernel,
        grid=(depth, n // bn),
        in_specs=[
            pl.BlockSpec((rows, d), lambda l, j: (0, 0)),
            pl.BlockSpec((None, d, bn), lambda l, j: (l, 0, j)),
            pl.BlockSpec((None, 1, bn), lambda l, j: (l, 0, j)),
        ],
        out_specs=pl.BlockSpec((None, rows, bn), lambda l, j: (l, 0, j)),
        out_shape=jax.ShapeDtypeStruct((depth, rows, n), F32),
        compiler_params=pltpu.CompilerParams(dimension_semantics=("arbitrary", "arbitrary")),
        name="modulation",
    )(cc, mod_w, mod_b.reshape(depth, 1, n))


def _even_proj_kernel(rotate, *refs):
    (h_ref, mod_ref, g1_ref, w_row_ref, w_t_ref, wq_up_ref, wk_up_ref, wv_up_ref,
     qa_g_ref, ka_g_ref, qlat_g_ref, kvlat_gc_ref, kvlat_gr_ref) = refs[:13]
    if rotate:
        (cos_ht_ref, sin_ht_ref, cos_rt_ref, sin_rt_ref,
         cos_k_ref, sin_k_ref, cos_kr_ref, sin_kr_ref) = refs[13:21]
    qa_ref, ka_ref, va_ref, qb_ref, kb_ref, vb_ref = refs[-6:]
    tile = TILE
    lane = lax.broadcasted_iota(jnp.int32, (tile, LANES), 1)
    head0 = lane < HEAD_DIM
    for s in range(h_ref.shape[0] // tile):
        rows = slice(s * tile, (s + 1) * tile)
        z = _norm_modulate(h_ref[rows, :], g1_ref[...], mod_ref[0:1, :], mod_ref[1:2, :]).astype(BF16)
        o = A_Q + A_KV
        ht_cq = _dot_nt(w_t_ref[o:o + B_Q_RANK, :], z)
        ht_ckv = _dot_nt(w_t_ref[o + B_Q_RANK:, :], z)
        hr = _dot(z, w_row_ref[...])
        ht_a = _dot_nt(w_t_ref[0:o, :], z)

        qa = ht_a[0:A_Q].reshape(N_HEADS_HALF, HEAD_DIM, tile)
        r = lax.rsqrt(jnp.mean(qa * qa, axis=1, keepdims=True) + NORM_EPS)
        qa = qa * r * _lane_tile(qa_g_ref[...], tile)
        if rotate:
            qa = _rope_rows(qa, cos_ht_ref[s], sin_ht_ref[s])
        qa_ref[s] = qa.reshape(A_Q, tile).astype(BF16)

        ka = hr[:, 0:A_KV]
        sq = ka * ka
        ss0 = jnp.sum(jnp.where(head0, sq, 0.0), axis=-1, keepdims=True)
        ss1 = jnp.sum(jnp.where(head0, 0.0, sq), axis=-1, keepdims=True)
        r = lax.rsqrt(jnp.where(head0, ss0, ss1) * (1.0 / HEAD_DIM) + NORM_EPS)
        ka = ka * r * ka_g_ref[...]
        if rotate:
            ka = _rope_lanes(ka, cos_k_ref[s], sin_k_ref[s], HEAD_DIM // 2,
                             (lane & (HEAD_DIM - 1)) < HEAD_DIM // 2)
        ka_ref[s] = ka.astype(BF16)

        va_ref[:, rows] = ht_a[A_Q:A_Q + A_KV].astype(BF16)

        cq = ht_cq
        r = lax.rsqrt(jnp.mean(cq * cq, axis=0, keepdims=True) + NORM_EPS)
        cq = (cq * r * _lane_tile(qlat_g_ref[...], tile)).astype(BF16)
        qb = _dot(wq_up_ref[...], cq).reshape(N_HEADS_HALF, LANES, tile)
        if rotate:
            q_rope = _rope_rows(qb[:, B_NOPE:B_QK, :], cos_rt_ref[s], sin_rt_ref[s])
            qb = jnp.concatenate([qb[:, :B_NOPE, :], q_rope, qb[:, B_QK:, :]], axis=1)
        qb = qb * (B_QK ** -0.5 * LOG2_E)
        qb_ref[s] = qb.reshape(N_HEADS_HALF * LANES, tile).astype(BF16)

        ckv = hr[:, A_KV:A_KV + B_KV_RANK]
        r = lax.rsqrt(jnp.mean(ckv * ckv, axis=-1, keepdims=True) + NORM_EPS)
        ckv = (ckv * r * kvlat_gr_ref[...]).astype(BF16)
        k_nope = _dot(ckv, wk_up_ref[...])
        kr = hr[:, A_KV + B_KV_RANK:A_KV + B_KV_RANK + LANES]
        if rotate:
            kr = _rope_lanes(kr, cos_kr_ref[s], sin_kr_ref[s], B_ROPE // 2, lane < B_NOPE + B_ROPE // 2)
        for hd in range(N_HEADS_HALF):
            kb_ref[s, hd] = (k_nope[:, hd * LANES:(hd + 1) * LANES] + kr).astype(BF16)

        ckvt = ht_ckv
        r = lax.rsqrt(jnp.mean(ckvt * ckvt, axis=0, keepdims=True) + NORM_EPS)
        ckvt = (ckvt * r * _lane_tile(kvlat_gc_ref[...], tile)).astype(BF16)
        vb_ref[:, rows] = _dot(wv_up_ref[...], ckvt).astype(BF16)


def _even_proj(h, mod, shared_mod, g1, wts, tabs, step):
    bsz, seq, d = h.shape
    n_sub = step // TILE
    n_tiles = seq // TILE
    rotate = tabs is not None

    def tiles_spec(*dims):
        nd = len(dims)
        return pl.BlockSpec((None, n_sub) + dims, lambda b, t, _n=nd: (b, t) + (0,) * _n)

    def seq_spec(rows):
        return pl.BlockSpec((None, rows, step), lambda b, t: (b, 0, t))

    in_specs = [pl.BlockSpec((None, step, d), lambda b, t: (b, t, 0)), _mod_spec(mod, shared_mod),
                _const_spec(g1.shape)]
    in_specs += [_const_spec(w.shape) for w in wts]
    if rotate:
        in_specs += [pl.BlockSpec((n_sub,) + tb.shape[1:], lambda b, t: (t, 0, 0)) for tb in tabs]
    out_specs = [
        tiles_spec(A_Q, TILE), tiles_spec(TILE, A_KV), seq_spec(A_KV),
        tiles_spec(N_HEADS_HALF * LANES, TILE), tiles_spec(N_HEADS_HALF, TILE, LANES),
        seq_spec(N_HEADS_HALF * B_V),
    ]
    out_shapes = [
        (bsz, n_tiles, A_Q, TILE), (bsz, n_tiles, TILE, A_KV), (bsz, A_KV, seq),
        (bsz, n_tiles, N_HEADS_HALF * LANES, TILE), (bsz, n_tiles, N_HEADS_HALF, TILE, LANES),
        (bsz, N_HEADS_HALF * B_V, seq),
    ]
    return pl.pallas_call(
        functools.partial(_even_proj_kernel, rotate),
        grid=(bsz, seq // step),
        in_specs=in_specs,
        out_specs=out_specs,
        out_shape=[jax.ShapeDtypeStruct(shp, BF16) for shp in out_shapes],
        compiler_params=_params(),
        name="even_proj",
    )(h, mod, g1, *wts, *(tabs if rotate else ()))


def _scores(key_parts, q, s_ref):
    tile = q.shape[1]
    off = 0
    m = None
    for keys in key_parts:
        n = keys.shape[0]
        s = _dot(keys, q)
        s_ref[off:off + n, :] = s
        part = jnp.max(s.reshape(n // SUBLANES, SUBLANES, tile), axis=0)
        m = part if m is None else jnp.maximum(m, part)
        off += n
    return jnp.max(m, axis=0, keepdims=True)


def _weighted_values(s_ref, n_keys, m, value_chunk, dv):
    tile = s_ref.shape[1]
    groups = TILE // SUBLANES
    l = jnp.zeros((SUBLANES, tile), F32)
    acc = jnp.zeros((dv, tile), F32)
    for c in range(n_keys // TILE):
        p = jnp.exp2(s_ref[c * TILE:(c + 1) * TILE, :] - m)
        l = l + jnp.sum(p.reshape(groups, SUBLANES, tile), axis=0)
        acc = acc + _dot(value_chunk(c), p.astype(BF16))
    l = jnp.sum(l, axis=0, keepdims=True)
    return acc * (1.0 / l)


def _even_attn_kernel(with_latent_keys, n_buffers, *refs):
    h_ref, mod_ref, qa_ref, qb_ref, kac_ref, vac_ref, kbc_ref, vbc_ref = refs[:8]
    if with_latent_keys:
        kal_ref, val_ref, kbl_ref, vbl_ref = refs[8:12]
    w_out_ref, o_ref = refs[-3 - n_buffers:-1 - n_buffers]
    s_refs = refs[-1 - n_buffers:-1]
    y_ref = refs[-1]
    n_ctx = kac_ref.shape[0] * TILE
    n_keys = n_ctx + (kal_ref.shape[0] * TILE if with_latent_keys else 0)

    def rows_of(index, size):
        return slice(index * size, (index + 1) * size)

    def value_chunk(c, rows, ctx_ref, lat_ref):
        if c * TILE < n_ctx:
            return ctx_ref[rows, c * TILE:(c + 1) * TILE]
        return lat_ref[rows, c * TILE - n_ctx:(c + 1) * TILE - n_ctx]

    def a_scores(qt, hd, s_ref):
        q = qa_ref[qt, rows_of(hd, HEAD_DIM), :]
        zeros = jnp.zeros_like(q)
        q = jnp.concatenate([q, zeros] if hd < GQA_GROUP else [zeros, q], axis=0)
        parts = [kac_ref[...].reshape(n_ctx, A_KV)]
        if with_latent_keys:
            parts.append(kal_ref[...].reshape(n_keys - n_ctx, A_KV))
        return _scores(parts, q, s_ref)

    def a_output(qt, hd, s_ref, m):
        v_rows = rows_of(hd // GQA_GROUP, HEAD_DIM)
        out = _weighted_values(s_ref, n_keys, m,
                               lambda c: value_chunk(c, v_rows, vac_ref, val_ref if with_latent_keys else None),
                               HEAD_DIM)
        y_ref[qt, rows_of(hd, HEAD_DIM), :] = out.astype(BF16)

    def b_scores(qt, hd, s_ref):
        parts = [kbc_ref[:, hd].reshape(n_ctx, LANES)]
        if with_latent_keys:
            parts.append(kbl_ref[:, hd].reshape(n_keys - n_ctx, LANES))
        return _scores(parts, qb_ref[qt, rows_of(hd, LANES), :], s_ref)

    def b_output(qt, hd, s_ref, m):
        v_rows = rows_of(hd, B_V)
        out = _weighted_values(s_ref, n_keys, m,
                               lambda c: value_chunk(c, v_rows, vbc_ref, vbl_ref if with_latent_keys else None),
                               B_V)
        y_ref[qt, rows_of(N_HEADS_HALF + hd, B_V), :] = out.astype(BF16)

    heads = []
    for qt in range(qa_ref.shape[0]):
        heads += [(a_scores, a_output, qt, hd) for hd in range(N_HEADS_HALF)]
        heads += [(b_scores, b_output, qt, hd) for hd in range(N_HEADS_HALF)]
    lookahead = len(s_refs) - 1
    maxes = {i: heads[i][0](heads[i][2], heads[i][3], s_refs[i % len(s_refs)]) for i in range(lookahead)}
    for i, (_, output, qt, hd) in enumerate(heads):
        j = i + lookahead
        if j < len(heads):
            maxes[j] = heads[j][0](heads[j][2], heads[j][3], s_refs[j % len(s_refs)])
        output(qt, hd, s_refs[i % len(s_refs)], maxes.pop(i))
        if i + 1 == len(heads) or heads[i + 1][2] != qt:
            rows = rows_of(qt, TILE)
            o_ref[rows, :] = h_ref[rows, :] + mod_ref[2:3, :] * _dot_tn(y_ref[qt], w_out_ref[...])


def _even_attn(h, mod, shared_mod, q, ctx_kv, lat_kv, w_out, step):
    bsz, seq, d = h.shape
    n_q = step // TILE
    qa, qb = q
    with_latent_keys = lat_kv is not None
    n_buffers = N_SCORE_BUFFERS if with_latent_keys else N_SCORE_BUFFERS_CTX
    kv = list(ctx_kv) + (list(lat_kv) if with_latent_keys else [])
    n_keys = sum(arr.shape[1] for arr in (ctx_kv[0],) + ((lat_kv[0],) if with_latent_keys else ())) * TILE

    def tile_spec(arr):
        return pl.BlockSpec((None, n_q) + arr.shape[2:], lambda b, t: (b, t, 0, 0))

    return pl.pallas_call(
        functools.partial(_even_attn_kernel, with_latent_keys, n_buffers),
        grid=(bsz, seq // step),
        in_specs=[pl.BlockSpec((None, step, d), lambda b, t: (b, t, 0)), _mod_spec(mod, shared_mod),
                  tile_spec(qa), tile_spec(qb)] + [_batch_spec(arr) for arr in kv] + [_const_spec(w_out.shape)],
        out_specs=pl.BlockSpec((None, step, d), lambda b, t: (b, t, 0)),
        out_shape=jax.ShapeDtypeStruct((bsz, seq, d), F32),
        scratch_shapes=[pltpu.VMEM((n_keys, TILE), F32)] * n_buffers
        + [pltpu.VMEM((n_q, D_MODEL, TILE), BF16)],
        compiler_params=_params(),
        name="even_attn",
    )(h, mod, qa, qb, *kv, w_out)


def _conv_ffn_kernel(final_norm, *refs):
    h_ref, prev_ref, next_ref, mod_ref, g2_ref, w_up_ref, cw_ref, cb_ref, w_down_ref = refs[:9]
    if final_norm:
        gf_ref = refs[9]
    o_ref = refs[-2 - 2 * FFN_BUFFERS]
    hs_flat = refs[-1 - 2 * FFN_BUFFERS:-1]
    act_ref = refs[-1]
    t = pl.program_id(1)
    last_tile = pl.num_programs(1) - 1
    tile = h_ref.shape[0]
    n_chunks = D_FF // FF_CHUNK
    g2 = g2_ref[...]
    shift, scale = mod_ref[3:4, :], mod_ref[4:5, :]
    has_prev = (t != 0).astype(F32)
    has_next = (t != last_tile).astype(F32)
    h = h_ref[...]
    z = jnp.concatenate([
        _norm_modulate(h, g2, shift, scale),
        _norm_modulate(prev_ref[...], g2, shift, scale) * has_prev,
        _norm_modulate(next_ref[...], g2, shift, scale) * has_next,
    ], axis=0).astype(BF16)

    hs_refs = [hs_flat[2 * i:2 * i + 2] for i in range(FFN_BUFFERS)]
    groups = tile // SUBLANES
    sub = lax.broadcasted_iota(jnp.int32, (groups, SUBLANES, FF_CHUNK), 1)

    def up(j):
        for hs_ref, col in zip(hs_refs[j % FFN_BUFFERS], (j * FF_CHUNK, D_FF + j * FF_CHUNK)):
            hs_ref[...] = _dot(z, w_up_ref[:, col:col + FF_CHUNK])

    def conv(hs_ref, col):
        hs = hs_ref[0:tile, :].reshape(groups, SUBLANES, FF_CHUNK)
        before = hs_ref[tile:tile + SUBLANES, :].reshape(1, SUBLANES, FF_CHUNK)
        after = hs_ref[tile + SUBLANES:tile + 2 * SUBLANES, :].reshape(1, SUBLANES, FF_CHUNK)
        down = pltpu.roll(jnp.concatenate([before, hs], axis=0), 1, 1)
        up_ = pltpu.roll(jnp.concatenate([hs, after], axis=0), SUBLANES - 1, 1)
        below = jnp.where(sub == 0, down[:groups], down[1:])
        above = jnp.where(sub == SUBLANES - 1, up_[1:], up_[:groups])
        cw = cw_ref[:, col:col + FF_CHUNK]
        return below * cw[0:1, :] + hs * cw[1:2, :] + above * cw[2:3, :] + cb_ref[:, col:col + FF_CHUNK]

    def gate(j):
        a = conv(hs_refs[j % FFN_BUFFERS][0], j * FF_CHUNK)
        g = conv(hs_refs[j % FFN_BUFFERS][1], D_FF + j * FF_CHUNK)
        act = (g * jax.nn.sigmoid(g) * a).reshape(tile, FF_CHUNK)
        act_ref[:, j * FF_CHUNK:(j + 1) * FF_CHUNK] = act.astype(BF16)

    lookahead = FFN_BUFFERS - 1
    for j in range(lookahead):
        up(j)
    for j in range(n_chunks):
        if j + lookahead < n_chunks:
            up(j + lookahead)
        gate(j)
    out = h + mod_ref[5:6, :] * _dot(act_ref[...], w_down_ref[...])
    if final_norm:
        ms = jnp.mean(out * out, axis=-1, keepdims=True)
        out = out * lax.rsqrt(ms + NORM_EPS) * gf_ref[...]
    o_ref[...] = out


def _conv_ffn(h, mod, shared_mod, g2, w_up, conv_w, conv_b, w_down, step, final_g=None):
    bsz, seq, d = h.shape
    rows_per_step = step // SUBLANES
    last_block = seq // SUBLANES - 1
    final_norm = final_g is not None

    def weight_spec(w):
        return pl.BlockSpec(w.shape, lambda b, t, _n=w.ndim: (0,) * _n, pipeline_mode=pl.Buffered(1))

    in_specs = [
        pl.BlockSpec((None, step, d), lambda b, t: (b, t, 0)),
        pl.BlockSpec((None, SUBLANES, d), lambda b, t: (b, jnp.maximum(t * rows_per_step - 1, 0), 0)),
        pl.BlockSpec((None, SUBLANES, d),
                     lambda b, t: (b, jnp.minimum((t + 1) * rows_per_step, last_block), 0)),
        _mod_spec(mod, shared_mod),
        _const_spec(g2.shape), weight_spec(w_up), _const_spec(conv_w.shape),
        _const_spec(conv_b.shape), weight_spec(w_down),
    ]
    args = [h, h, h, mod, g2, w_up, conv_w, conv_b, w_down]
    if final_norm:
        in_specs.append(_const_spec(final_g.shape))
        args.append(final_g)
    return pl.pallas_call(
        functools.partial(_conv_ffn_kernel, final_norm),
        grid=(bsz, seq // step),
        in_specs=in_specs,
        out_specs=pl.BlockSpec((None, step, d), lambda b, t: (b, t, 0)),
        out_shape=jax.ShapeDtypeStruct((bsz, seq, d), F32),
        scratch_shapes=[pltpu.VMEM((step + 2 * SUBLANES, FF_CHUNK), F32)] * (2 * FFN_BUFFERS)
        + [pltpu.VMEM((step, D_FF), BF16)],
        compiler_params=_params(),
        name="conv_ffn",
    )(*args)


def _gelu(x):
    return 0.5 * x * (1.0 + jnp.tanh(0.7978845608028654 * (x + 0.044715 * (x * x * x))))


def _odd_proj_kernel(h_ref, mod_ref, g1_ref, w_row_ref, w_t_ref, ln_g_ref, ln_b_ref, ws_ref, bs_ref,
                     cos_ht_ref, sin_ht_ref, cos_k_ref, sin_k_ref,
                     yc_ref, qd_ref, kd_ref, vd_ref):
    tile = TILE
    chunks = tile // C_CHUNK
    lane = lax.broadcasted_iota(jnp.int32, (tile, LANES), 1)
    gw = C_WIDTH // C_GROUPS
    for s in range(h_ref.shape[0] // tile):
        z = _norm_modulate(h_ref[s * tile:(s + 1) * tile, :], g1_ref[...],
                           mod_ref[0:1, :], mod_ref[1:2, :]).astype(BF16)
        ht_v = _dot_nt(w_t_ref[C_WIDTH:2 * C_WIDTH, :], z)
        ht_u = _dot_nt(w_t_ref[0:C_WIDTH, :], z)
        ht_d = _dot_nt(w_t_ref[2 * C_WIDTH:, :], z)
        hr = _dot(z, w_row_ref[...])

        v = _gelu(ht_v)
        mu = jnp.mean(v, axis=0, keepdims=True)
        vc = v - mu
        var = jnp.mean(vc * vc, axis=0, keepdims=True)
        vn = vc * lax.rsqrt(var + NORM_EPS) * _lane_tile(ln_g_ref[...], tile) + _lane_tile(ln_b_ref[...], tile)
        vn = vn.astype(BF16)
        for g in range(C_GROUPS):
            for n in range(chunks):
                cols = slice(n * C_CHUNK, (n + 1) * C_CHUNK)
                rows = slice(g * gw, (g + 1) * gw)
                mixed = _dot(vn[rows, cols], ws_ref[g]) + bs_ref[g]
                yc_ref[s, rows, cols] = (_gelu(ht_u[rows, cols]) * mixed).astype(BF16)

        qd = ht_d[0:A_Q].reshape(N_HEADS_HALF, HEAD_DIM, tile)
        qd = _rope_rows(qd, cos_ht_ref[s], sin_ht_ref[s]) * (HEAD_DIM ** -0.5 * LOG2_E)
        qd_ref[s] = qd.reshape(A_Q, tile).astype(BF16)
        kd = _rope_lanes(hr, cos_k_ref[s], sin_k_ref[s], HEAD_DIM // 2,
                         (lane & (HEAD_DIM - 1)) < HEAD_DIM // 2).astype(BF16)
        vd = ht_d[A_Q:A_Q + A_KV].astype(BF16)
        for n in range(chunks):
            kd_ref[s * chunks + n] = kd[n * C_CHUNK:(n + 1) * C_CHUNK, :]
            vd_ref[s * chunks + n] = vd[:, n * C_CHUNK:(n + 1) * C_CHUNK]


def _odd_proj(h, mod, g1, wts, tabs, step):
    bsz, seq, d = h.shape
    n_sub = step // TILE
    n_tiles = seq // TILE
    chunks = TILE // C_CHUNK

    def tiles_spec(rows):
        return pl.BlockSpec((None, n_sub, rows, TILE), lambda b, t: (b, t, 0, 0))

    chunk_spec = pl.BlockSpec((None, n_sub * chunks, C_CHUNK, C_CHUNK), lambda b, t: (b, t, 0, 0))
    in_specs = [pl.BlockSpec((None, step, d), lambda b, t: (b, t, 0)), _mod_spec(mod, False),
                _const_spec(g1.shape)]
    in_specs += [_const_spec(w.shape) for w in wts]
    in_specs += [pl.BlockSpec((n_sub,) + tb.shape[1:], lambda b, t: (t, 0, 0)) for tb in tabs]
    return pl.pallas_call(
        _odd_proj_kernel,
        grid=(bsz, seq // step),
        in_specs=in_specs,
        out_specs=[tiles_spec(C_WIDTH), tiles_spec(A_Q), chunk_spec, chunk_spec],
        out_shape=[jax.ShapeDtypeStruct((bsz, n_tiles, C_WIDTH, TILE), BF16),
                   jax.ShapeDtypeStruct((bsz, n_tiles, A_Q, TILE), BF16),
                   jax.ShapeDtypeStruct((bsz, n_tiles * chunks, C_CHUNK, A_KV), BF16),
                   jax.ShapeDtypeStruct((bsz, n_tiles * chunks, A_KV, C_CHUNK), BF16)],
        compiler_params=_params(),
        name="odd_proj",
    )(h, mod, g1, *wts, *tabs)


def _odd_kv_kernel(h_ref, mod_ref, g1_ref, w_k_ref, w_vt_ref, kd_ref, vd_ref):
    z = _norm_modulate(h_ref[...], g1_ref[...], mod_ref[0:1, :], mod_ref[1:2, :]).astype(BF16)
    kd = _dot(z, w_k_ref[...]).astype(BF16)
    vd = _dot_nt(w_vt_ref[...], z).astype(BF16)
    for n in range(h_ref.shape[0] // C_CHUNK):
        kd_ref[n] = kd[n * C_CHUNK:(n + 1) * C_CHUNK, :]
        vd_ref[n] = vd[:, n * C_CHUNK:(n + 1) * C_CHUNK]


def _odd_kv(h, mod, g1, w_k, w_vt):
    bsz, seq, d = h.shape
    chunks = seq // C_CHUNK
    chunk_spec = pl.BlockSpec((None, chunks, C_CHUNK, C_CHUNK), lambda b, t: (b, 0, 0, 0))
    return pl.pallas_call(
        _odd_kv_kernel,
        grid=(bsz, 1),
        in_specs=[pl.BlockSpec((None, seq, d), lambda b, t: (b, 0, 0)), _mod_spec(mod, True),
                  _const_spec(g1.shape), _const_spec(w_k.shape), _const_spec(w_vt.shape)],
        out_specs=[chunk_spec, chunk_spec],
        out_shape=[jax.ShapeDtypeStruct((bsz, chunks, C_CHUNK, A_KV), BF16),
                   jax.ShapeDtypeStruct((bsz, chunks, A_KV, C_CHUNK), BF16)],
        compiler_params=_params(),
        name="odd_kv",
    )(h, mod, g1, w_k, w_vt)


def _window_attn_kernel(sink_ref, h_ref, mod_ref, yc_ref, qd_ref, kdc_ref, vdc_ref, kdl_ref, vdl_ref, w_out_ref,
                        o_ref, *scratch):
    s_refs = scratch[:N_WINDOW_BUFFERS]
    bias_ref, y_ref = scratch[N_WINDOW_BUFFERS:]
    t = pl.program_id(1)
    tile = qd_ref.shape[1]
    ctx_chunks = kdc_ref.shape[0]
    lat_chunks = kdl_ref.shape[0]
    sub = tile // C_CHUNK
    win_chunks = sub + 2
    w0 = jnp.clip(t * sub - 1, 0, lat_chunks - win_chunks)
    n_ctx = ctx_chunks * C_CHUNK
    n_win = win_chunks * C_CHUNK
    pair = 2 * C_CHUNK

    k_ctx = kdc_ref[...].reshape(n_ctx, A_KV)
    k_win = kdl_ref[pl.ds(w0, win_chunks)].reshape(n_win, A_KV)
    keys = jnp.concatenate([k_ctx, k_win], axis=0)
    key_pos = w0 * C_CHUNK + lax.broadcasted_iota(jnp.int32, (n_win, tile), 0)
    qry_pos = t * tile + lax.broadcasted_iota(jnp.int32, (n_win, tile), 1)
    bias_ref[...] = jnp.where(jnp.abs(qry_pos - key_pos) <= WINDOW, 0.0, NEG_INF)
    zeros = jnp.zeros((HEAD_DIM, tile), BF16)
    ones = jnp.ones((BF16_ROWS, pair), BF16)

    def scores(hd):
        s_ref = s_refs[hd % len(s_refs)]
        q = qd_ref[hd * HEAD_DIM:(hd + 1) * HEAD_DIM, :]
        q = jnp.concatenate([q, zeros] if hd < GQA_GROUP else [zeros, q], axis=0)
        s = _dot(keys, q)
        s_ctx = s[:n_ctx]
        s_win = s[n_ctx:] + bias_ref[...]
        s_ref[0:n_ctx, :] = s_ctx
        s_ref[n_ctx:, :] = s_win
        m = jnp.maximum(jnp.max(s_ctx.reshape(n_ctx // SUBLANES, SUBLANES, tile), axis=0),
                        jnp.max(s_win.reshape(n_win // SUBLANES, SUBLANES, tile), axis=0))
        return jnp.maximum(jnp.max(m, axis=0, keepdims=True), sink_ref[hd] * LOG2_E)

    def output(hd, m):
        s_ref = s_refs[hd % len(s_refs)]
        rows = slice((hd // GQA_GROUP) * HEAD_DIM, (hd // GQA_GROUP + 1) * HEAD_DIM)
        acc = jnp.zeros((HEAD_DIM + BF16_ROWS, tile), F32)
        for c in range((n_ctx + n_win) // pair):
            p = jnp.exp2(s_ref[c * pair:(c + 1) * pair, :] - m)
            if 2 * c < ctx_chunks:
                v = [vdc_ref[2 * c, rows, :], vdc_ref[2 * c + 1, rows, :]]
            else:
                first = w0 + (2 * c - ctx_chunks)
                v = [vdl_ref[first, rows, :], vdl_ref[first + 1, rows, :]]
            v = jnp.concatenate([jnp.concatenate(v, axis=1), ones], axis=0)
            acc = acc + _dot(v, p.astype(BF16))
        l = acc[HEAD_DIM:HEAD_DIM + 1] + jnp.exp2(sink_ref[hd] * LOG2_E - m)
        y_ref[hd * HEAD_DIM:(hd + 1) * HEAD_DIM, :] = (acc[:HEAD_DIM] * (1.0 / l)).astype(BF16)

    lookahead = len(s_refs) - 1
    maxes = {hd: scores(hd) for hd in range(lookahead)}
    for hd in range(N_HEADS_HALF):
        if hd + lookahead < N_HEADS_HALF:
            maxes[hd + lookahead] = scores(hd + lookahead)
        output(hd, maxes.pop(hd))

    y = _dot_tn(yc_ref[...], w_out_ref[0:C_WIDTH, :]) + _dot_tn(y_ref[...], w_out_ref[C_WIDTH:, :])
    o_ref[...] = h_ref[...] + mod_ref[2:3, :] * y


def _window_attn(sink, h, mod, yc, qd, ctx_kv, lat_kv, w_out):
    bsz, seq, d = h.shape

    def tile_spec(rows):
        return pl.BlockSpec((None, None, rows, TILE), lambda b, t: (b, t, 0, 0))

    kv = list(ctx_kv) + list(lat_kv)
    return pl.pallas_call(
        _window_attn_kernel,
        grid=(bsz, seq // TILE),
        in_specs=[pl.BlockSpec(memory_space=pltpu.SMEM),
                  pl.BlockSpec((None, TILE, d), lambda b, t: (b, t, 0)), _mod_spec(mod, False),
                  tile_spec(C_WIDTH), tile_spec(A_Q)] + [_batch_spec(arr) for arr in kv]
        + [_const_spec(w_out.shape)],
        out_specs=pl.BlockSpec((None, TILE, d), lambda b, t: (b, t, 0)),
        out_shape=jax.ShapeDtypeStruct((bsz, seq, d), F32),
        scratch_shapes=[pltpu.VMEM((2 * TILE + 2 * C_CHUNK, TILE), F32)] * N_WINDOW_BUFFERS
        + [pltpu.VMEM((TILE + 2 * C_CHUNK, TILE), F32), pltpu.VMEM((A_Q, TILE), BF16)],
        compiler_params=_params(),
        name="window_attn",
    )(sink, h, mod, yc, qd, *kv, w_out)


def _rope_tables(seq):
    def base(dim):
        n_rows = seq // GRID_W
        rows = np.repeat(np.arange(n_rows), GRID_W).astype(np.float64)
        cols = np.tile(np.arange(GRID_W), n_rows).astype(np.float64)
        quarter = dim // 4
        inv_freq = ROPE_THETA ** (-np.arange(quarter, dtype=np.float64) / quarter)
        ang = np.concatenate([rows[:, None] * inv_freq, cols[:, None] * inv_freq], axis=-1)
        return np.cos(ang), np.sin(ang)

    n_tiles = seq // TILE

    def feature_major(tb):
        return jnp.asarray(tb.reshape(n_tiles, TILE, tb.shape[1]).transpose(0, 2, 1), F32)

    def token_major(tb):
        return jnp.asarray(tb.reshape(n_tiles, TILE, LANES), F32)

    cos_h, sin_h = base(HEAD_DIM)
    cos_r, sin_r = base(B_ROPE)
    cos_k = np.tile(cos_h, (1, LANES // (HEAD_DIM // 2)))
    sin_k = np.tile(np.concatenate([-sin_h, sin_h], axis=1), (1, LANES // HEAD_DIM))
    pad_l = np.zeros((seq, B_NOPE))
    pad_r = np.zeros((seq, LANES - B_QK))
    cos_kr = np.concatenate([pad_l, cos_r, cos_r, pad_r], axis=1)
    sin_kr = np.concatenate([pad_l, -sin_r, sin_r, pad_r], axis=1)
    return dict(cos_ht=feature_major(cos_h), sin_ht=feature_major(sin_h),
                cos_rt=feature_major(cos_r), sin_rt=feature_major(sin_r),
                cos_k=token_major(cos_k), sin_k=token_major(sin_k),
                cos_kr=token_major(cos_kr), sin_kr=token_major(sin_kr))


def _col(v):
    return jnp.broadcast_to(v.astype(F32)[:, None], (v.shape[0], LANES))


def _ffn_weights(w_up, conv_w, conv_b, w_down):
    return w_up.astype(BF16), conv_w, conv_b[None, :], w_down.astype(BF16)


def kernel(x, c, ctx, c_ctx, mod_w, mod_b, norm1_g, norm2_g, ev_w_in, ev_qa_g, ev_ka_g, ev_qlat_g, ev_w_q_up, ev_kvlat_g, ev_w_kv_up, ev_w_out, od_w_in, od_ln_g, od_ln_b, od_sgu_w, od_sgu_b, od_sink, od_w_out, ffn_up, ffn_conv_w, ffn_conv_b, ffn_down, final_g):
    bsz, seq, d = x.shape
    ctx_len = ctx.shape[1]
    assert d == D_MODEL and ctx_len % TILE == 0 and seq % FFN_STEP == 0 and mod_w.shape[0] == 2

    n_rows = -(-(bsz + 1) // SUBLANES) * SUBLANES
    cc = jnp.concatenate([c, jnp.zeros((n_rows - bsz - 1, d), F32), c_ctx[None, :]], axis=0)
    mods = _modulation(cc, mod_w, mod_b).reshape(2, n_rows, 6, d)
    tabs = _rope_tables(seq)

    w_in = ev_w_in[0]
    s0, s1, s2, s3, s4 = A_Q, A_Q + A_KV, A_Q + 2 * A_KV, A_Q + 2 * A_KV + B_Q_RANK, A_Q + 2 * A_KV + B_Q_RANK + B_KV_RANK
    w_kr = jnp.zeros((d, LANES), F32).at[:, B_NOPE:B_QK].set(w_in[:, s4:])
    w_row = jnp.concatenate([w_in[:, s0:s1], w_in[:, s3:s4], w_kr], axis=1).astype(BF16)
    w_t = jnp.concatenate([w_in[:, :s0], w_in[:, s1:s2], w_in[:, s2:s3], w_in[:, s3:s4]], axis=1).T.astype(BF16)
    wq = ev_w_q_up[0].reshape(B_Q_RANK, N_HEADS_HALF, B_QK)
    wq = jnp.pad(wq, ((0, 0), (0, 0), (0, LANES - B_QK))).reshape(B_Q_RANK, N_HEADS_HALF * LANES).T.astype(BF16)
    wkv = ev_w_kv_up[0].reshape(B_KV_RANK, N_HEADS_HALF, B_NOPE + B_V)
    wk = jnp.pad(wkv[:, :, :B_NOPE], ((0, 0), (0, 0), (0, LANES - B_NOPE)))
    wk = wk.reshape(B_KV_RANK, N_HEADS_HALF * LANES).astype(BF16)
    wv = wkv[:, :, B_NOPE:].reshape(B_KV_RANK, N_HEADS_HALF * B_V).T.astype(BF16)
    even_wts = [w_row, w_t, wq, wk, wv,
                _col(ev_qa_g[0] * (HEAD_DIM ** -0.5 * LOG2_E)), jnp.tile(ev_ka_g[0], KV_HEADS)[None, :],
                _col(ev_qlat_g[0]), _col(ev_kvlat_g[0]), ev_kvlat_g[0][None, :]]
    even_tabs = [tabs[k] for k in ("cos_ht", "sin_ht", "cos_rt", "sin_rt", "cos_k", "sin_k", "cos_kr", "sin_kr")]
    g1 = norm1_g[0][None, :]
    qa_c, ka_c, va_c, qb_c, kb_c, vb_c = _even_proj(ctx, mods[0], True, g1, even_wts, None, ctx_len)
    qa_l, ka_l, va_l, qb_l, kb_l, vb_l = _even_proj(x, mods[0], False, g1, even_wts, even_tabs, LATENT_STEP)
    ctx_kv = (ka_c, va_c, kb_c, vb_c)
    w_out = ev_w_out[0].astype(BF16)
    hc = _even_attn(ctx, mods[0], True, (qa_c, qb_c), ctx_kv, None, w_out, TILE)
    hl = _even_attn(x, mods[0], False, (qa_l, qb_l), ctx_kv, (ka_l, va_l, kb_l, vb_l), w_out, TILE)
    ffn0 = _ffn_weights(ffn_up[0], ffn_conv_w[0], ffn_conv_b[0], ffn_down[0])
    g2 = norm2_g[0][None, :]
    hc = _conv_ffn(hc, mods[0], True, g2, *ffn0, step=ctx_len)
    hl = _conv_ffn(hl, mods[0], False, g2, *ffn0, step=FFN_STEP)

    w_in = od_w_in[0]
    o2, o3 = 2 * C_WIDTH + A_Q, 2 * C_WIDTH + A_Q + A_KV
    w_row = w_in[:, o2:o3].astype(BF16)
    w_t = jnp.concatenate([w_in[:, :o2], w_in[:, o3:]], axis=1).T.astype(BF16)
    ws_t = od_sgu_w[0].transpose(0, 2, 1).astype(BF16)
    odd_wts = [w_row, w_t, _col(od_ln_g[0]), _col(od_ln_b[0]), ws_t, od_sgu_b[0][:, None, :]]
    odd_tabs = [tabs[k] for k in ("cos_ht", "sin_ht", "cos_k", "sin_k")]
    g1 = norm1_g[1][None, :]
    kd_c, vd_c = _odd_kv(hc, mods[1], g1, w_row, w_in[:, o3:].T.astype(BF16))
    yc, qd, kd_l, vd_l = _odd_proj(hl, mods[1], g1, odd_wts, odd_tabs, LATENT_STEP)
    hl = _window_attn(od_sink[0], hl, mods[1], yc, qd, (kd_c, vd_c), (kd_l, vd_l), od_w_out[0].astype(BF16))
    return _conv_ffn(hl, mods[1], False, norm2_g[1][None, :],
                     *_ffn_weights(ffn_up[1], ffn_conv_w[1], ffn_conv_b[1], ffn_down[1]),
                     step=FFN_STEP, final_g=final_g[None, :])
```

```python
import functools

import jax
import jax.numpy as jnp
import numpy as np
from jax import lax
from jax.experimental import pallas as pl
from jax.experimental.pallas import tpu as pltpu

F32 = jnp.float32
BF16 = jnp.bfloat16

D_MODEL = 1024
HEAD_DIM = 64
GRID_W = 64
ROPE_THETA = 10000.0
NORM_EPS = 1e-6
NEG_INF = -1e30
LOG2_E = 1.4426950408889634
N_HEADS_HALF = 8
KV_HEADS = 2
GQA_GROUP = N_HEADS_HALF // KV_HEADS
A_Q = N_HEADS_HALF * HEAD_DIM
A_KV = KV_HEADS * HEAD_DIM
B_Q_RANK = 256
B_KV_RANK = 128
B_NOPE = 64
B_ROPE = 32
B_V = 64
B_QK = B_NOPE + B_ROPE
C_WIDTH = 512
C_GROUPS = 8
C_CHUNK = 128
WINDOW = 128
D_FF = 2816
LANES = 128
SUBLANES = 8
BF16_ROWS = 16
TILE = 256
LATENT_STEP = 1024
FFN_STEP = 512
FF_CHUNK = 256
FFN_BUFFERS = 2
N_SCORE_BUFFERS = 3
N_SCORE_BUFFERS_CTX = 6
VMEM_LIMIT = 56 * 1024 * 1024


def _dot(a, b):
    return jnp.dot(a, b, preferred_element_type=F32)


def _dot_nt(a, b):
    return lax.dot_general(a, b, (((1,), (1,)), ((), ())), preferred_element_type=F32)


def _dot_tn(a, b):
    return lax.dot_general(a, b, (((0,), (0,)), ((), ())), preferred_element_type=F32)


def _lane_tile(v, width):
    return jnp.concatenate([v] * (width // LANES), axis=1)


def _norm_modulate(h, g, shift, scale):
    ms = jnp.mean(h * h, axis=-1, keepdims=True)
    y = h * lax.rsqrt(ms + NORM_EPS) * g
    return y * (1.0 + scale) + shift


def _rope_rows(x, cos, sin):
    half = cos.shape[0]
    x1 = x[..., :half, :]
    x2 = x[..., half:, :]
    return jnp.concatenate([x1 * cos - x2 * sin, x1 * sin + x2 * cos], axis=-2)


def _rope_lanes(x, cos, sin_signed, half, first_half_mask):
    up = pltpu.roll(x, LANES - half, 1)
    down = pltpu.roll(x, half, 1)
    return x * cos + jnp.where(first_half_mask, up, down) * sin_signed


def _const_spec(shape):
    nd = len(shape)
    return pl.BlockSpec(shape, lambda b, t, _n=nd: (0,) * _n)


def _batch_spec(arr):
    nd = arr.ndim - 1
    return pl.BlockSpec((None,) + arr.shape[1:], lambda b, t, _n=nd: (b,) + (0,) * _n)


def _mod_spec(mod, shared_row):
    n_rows, _, d = mod.shape
    if shared_row:
        return pl.BlockSpec((None, 6, d), lambda b, t: (n_rows - 1, 0, 0))
    return pl.BlockSpec((None, 6, d), lambda b, t: (b, 0, 0))


def _params():
    return pltpu.CompilerParams(dimension_semantics=("parallel", "arbitrary"), vmem_limit_bytes=VMEM_LIMIT)


def _mod_kernel(c_ref, w_ref, b_ref, o_ref):
    c = c_ref[...]
    a = c * jax.nn.sigmoid(c)
    o_ref[...] = jnp.dot(a, w_ref[...], preferred_element_type=F32,
                         precision=lax.Precision.HIGHEST) + b_ref[...]


def _modulation(cc, mod_w, mod_b):
    depth, d, n = mod_w.shape
    rows = cc.shape[0]
    bn = 1024
    return pl.pallas_call(
        _mod_kernel,
        grid=(depth, n // bn),
        in_specs=[
            pl.BlockSpec((rows, d), lambda l, j: (0, 0)),
            pl.BlockSpec((None, d, bn), lambda l, j: (l, 0, j)),
            pl.BlockSpec((None, 1, bn), lambda l, j: (l, 0, j)),
        ],
        out_specs=pl.BlockSpec((None, rows, bn), lambda l, j: (l, 0, j)),
        out_shape=jax.ShapeDtypeStruct((depth, rows, n), F32),
        compiler_params=pltpu.CompilerParams(dimension_semantics=("arbitrary", "arbitrary")),
        name="modulation",
    )(cc, mod_w, mod_b.reshape(depth, 1, n))


def _even_proj_kernel(rotate, *refs):
    (h_ref, mod_ref, g1_ref, w_row_ref, w_t_ref, wq_up_ref, wk_up_ref, wv_up_ref,
     qa_g_ref, ka_g_ref, qlat_g_ref, kvlat_gc_ref, kvlat_gr_ref) = refs[:13]
    if rotate:
        (cos_ht_ref, sin_ht_ref, cos_rt_ref, sin_rt_ref,
         cos_k_ref, sin_k_ref, cos_kr_ref, sin_kr_ref) = refs[13:21]
    qa_ref, ka_ref, va_ref, qb_ref, kb_ref, vb_ref = refs[-6:]
    tile = TILE
    lane = lax.broadcasted_iota(jnp.int32, (tile, LANES), 1)
    head0 = lane < HEAD_DIM
    for s in range(h_ref.shape[0] // tile):
        rows = slice(s * tile, (s + 1) * tile)
        z = _norm_modulate(h_ref[rows, :], g1_ref[...], mod_ref[0:1, :], mod_ref[1:2, :]).astype(BF16)
        o = A_Q + A_KV
        ht_cq = _dot_nt(w_t_ref[o:o + B_Q_RANK, :], z)
        ht_ckv = _dot_nt(w_t_ref[o + B_Q_RANK:, :], z)
        hr = _dot(z, w_row_ref[...])
        ht_a = _dot_nt(w_t_ref[0:o, :], z)

        qa = ht_a[0:A_Q].reshape(N_HEADS_HALF, HEAD_DIM, tile)
        r = lax.rsqrt(jnp.mean(qa * qa, axis=1, keepdims=True) + NORM_EPS)
        qa = qa * r * _lane_tile(qa_g_ref[...], tile)
        if rotate:
            qa = _rope_rows(qa, cos_ht_ref[s], sin_ht_ref[s])
        qa_ref[s] = qa.reshape(A_Q, tile).astype(BF16)

        ka = hr[:, 0:A_KV]
        sq = ka * ka
        ss0 = jnp.sum(jnp.where(head0, sq, 0.0), axis=-1, keepdims=True)
        ss1 = jnp.sum(jnp.where(head0, 0.0, sq), axis=-1, keepdims=True)
        r = lax.rsqrt(jnp.where(head0, ss0, ss1) * (1.0 / HEAD_DIM) + NORM_EPS)
        ka = ka * r * ka_g_ref[...]
        if rotate:
            ka = _rope_lanes(ka, cos_k_ref[s], sin_k_ref[s], HEAD_DIM // 2,
                             (lane & (HEAD_DIM - 1)) < HEAD_DIM // 2)
        ka_ref[s] = ka.astype(BF16)

        va_ref[:, rows] = ht_a[A_Q:A_Q + A_KV].astype(BF16)

        cq = ht_cq
        r = lax.rsqrt(jnp.mean(cq * cq, axis=0, keepdims=True) + NORM_EPS)
        cq = (cq * r * _lane_tile(qlat_g_ref[...], tile)).astype(BF16)
        qb = _dot(wq_up_ref[...], cq).reshape(N_HEADS_HALF, LANES, tile)
        if rotate:
            q_rope = _rope_rows(qb[:, B_NOPE:B_QK, :], cos_rt_ref[s], sin_rt_ref[s])
            qb = jnp.concatenate([qb[:, :B_NOPE, :], q_rope, qb[:, B_QK:, :]], axis=1)
        qb = qb * (B_QK ** -0.5 * LOG2_E)
        qb_ref[s] = qb.reshape(N_HEADS_HALF * LANES, tile).astype(BF16)

        ckv = hr[:, A_KV:A_KV + B_KV_RANK]
        r = lax.rsqrt(jnp.mean(ckv * ckv, axis=-1, keepdims=True) + NORM_EPS)
        ckv = (ckv * r * kvlat_gr_ref[...]).astype(BF16)
        k_nope = _dot(ckv, wk_up_ref[...])
        kr = hr[:, A_KV + B_KV_RANK:A_KV + B_KV_RANK + LANES]
        if rotate:
            kr = _rope_lanes(kr, cos_kr_ref[s], sin_kr_ref[s], B_ROPE // 2, lane < B_NOPE + B_ROPE // 2)
        for hd in range(N_HEADS_HALF):
            kb_ref[s, hd] = (k_nope[:, hd * LANES:(hd + 1) * LANES] + kr).astype(BF16)

        ckvt = ht_ckv
        r = lax.rsqrt(jnp.mean(ckvt * ckvt, axis=0, keepdims=True) + NORM_EPS)
        ckvt = (ckvt * r * _lane_tile(kvlat_gc_ref[...], tile)).astype(BF16)
        vb_ref[:, rows] = _dot(wv_up_ref[...], ckvt).astype(BF16)


def _even_proj(h, mod, shared_mod, g1, wts, tabs, step):
    bsz, seq, d = h.shape
    n_sub = step // TILE
    n_tiles = seq // TILE
    rotate = tabs is not None

    def tiles_spec(*dims):
        nd = len(dims)
        return pl.BlockSpec((None, n_sub) + dims, lambda b, t, _n=nd: (b, t) + (0,) * _n)

    def seq_spec(rows):
        return pl.BlockSpec((None, rows, step), lambda b, t: (b, 0, t))

    in_specs = [pl.BlockSpec((None, step, d), lambda b, t: (b, t, 0)), _mod_spec(mod, shared_mod),
                _const_spec(g1.shape)]
    in_specs += [_const_spec(w.shape) for w in wts]
    if rotate:
        in_specs += [pl.BlockSpec((n_sub,) + tb.shape[1:], lambda b, t: (t, 0, 0)) for tb in tabs]
    out_specs = [
        tiles_spec(A_Q, TILE), tiles_spec(TILE, A_KV), seq_spec(A_KV),
        tiles_spec(N_HEADS_HALF * LANES, TILE), tiles_spec(N_HEADS_HALF, TILE, LANES),
        seq_spec(N_HEADS_HALF * B_V),
    ]
    out_shapes = [
        (bsz, n_tiles, A_Q, TILE), (bsz, n_tiles, TILE, A_KV), (bsz, A_KV, seq),
        (bsz, n_tiles, N_HEADS_HALF * LANES, TILE), (bsz, n_tiles, N_HEADS_HALF, TILE, LANES),
        (bsz, N_HEADS_HALF * B_V, seq),
    ]
    return pl.pallas_call(
        functools.partial(_even_proj_kernel, rotate),
        grid=(bsz, seq // step),
        in_specs=in_specs,
        out_specs=out_specs,
        out_shape=[jax.ShapeDtypeStruct(shp, BF16) for shp in out_shapes],
        compiler_params=_params(),
        name="even_proj",
    )(h, mod, g1, *wts, *(tabs if rotate else ()))


def _scores(key_parts, q, s_ref):
    tile = q.shape[1]
    off = 0
    m = None
    for keys in key_parts:
        n = keys.shape[0]
        s = _dot(keys, q)
        s_ref[off:off + n, :] = s
        part = jnp.max(s.reshape(n // SUBLANES, SUBLANES, tile), axis=0)
        m = part if m is None else jnp.maximum(m, part)
        off += n
    return jnp.max(m, axis=0, keepdims=True)


def _weighted_values(s_ref, n_keys, m, value_chunk, dv):
    tile = s_ref.shape[1]
    groups = TILE // SUBLANES
    l = jnp.zeros((SUBLANES, tile), F32)
    acc = jnp.zeros((dv, tile), F32)
    for c in range(n_keys // TILE):
        p = jnp.exp2(s_ref[c * TILE:(c + 1) * TILE, :] - m)
        l = l + jnp.sum(p.reshape(groups, SUBLANES, tile), axis=0)
        acc = acc + _dot(value_chunk(c), p.astype(BF16))
    l = jnp.sum(l, axis=0, keepdims=True)
    return acc * (1.0 / l)


def _even_attn_kernel(with_latent_keys, n_buffers, *refs):
    h_ref, mod_ref, qa_ref, qb_ref, kac_ref, vac_ref, kbc_ref, vbc_ref = refs[:8]
    if with_latent_keys:
        kal_ref, val_ref, kbl_ref, vbl_ref = refs[8:12]
    w_out_ref, o_ref = refs[-3 - n_buffers:-1 - n_buffers]
    s_refs = refs[-1 - n_buffers:-1]
    y_ref = refs[-1]
    n_ctx = kac_ref.shape[0] * TILE
    n_keys = n_ctx + (kal_ref.shape[0] * TILE if with_latent_keys else 0)

    def rows_of(index, size):
        return slice(index * size, (index + 1) * size)

    def value_chunk(c, rows, ctx_ref, lat_ref):
        if c * TILE < n_ctx:
            return ctx_ref[rows, c * TILE:(c + 1) * TILE]
        return lat_ref[rows, c * TILE - n_ctx:(c + 1) * TILE - n_ctx]

    def a_scores(qt, hd, s_ref):
        q = qa_ref[qt, rows_of(hd, HEAD_DIM), :]
        zeros = jnp.zeros_like(q)
        q = jnp.concatenate([q, zeros] if hd < GQA_GROUP else [zeros, q], axis=0)
        parts = [kac_ref[...].reshape(n_ctx, A_KV)]
        if with_latent_keys:
            parts.append(kal_ref[...].reshape(n_keys - n_ctx, A_KV))
        return _scores(parts, q, s_ref)

    def a_output(qt, hd, s_ref, m):
        v_rows = rows_of(hd // GQA_GROUP, HEAD_DIM)
        out = _weighted_values(s_ref, n_keys, m,
                               lambda c: value_chunk(c, v_rows, vac_ref, val_ref if with_latent_keys else None),
                               HEAD_DIM)
        y_ref[qt, rows_of(hd, HEAD_DIM), :] = out.astype(BF16)

    def b_scores(qt, hd, s_ref):
        parts = [kbc_ref[:, hd].reshape(n_ctx, LANES)]
        if with_latent_keys:
            parts.append(kbl_ref[:, hd].reshape(n_keys - n_ctx, LANES))
        return _scores(parts, qb_ref[qt, rows_of(hd, LANES), :], s_ref)

    def b_output(qt, hd, s_ref, m):
        v_rows = rows_of(hd, B_V)
        out = _weighted_values(s_ref, n_keys, m,
                               lambda c: value_chunk(c, v_rows, vbc_ref, vbl_ref if with_latent_keys else None),
                               B_V)
        y_ref[qt, rows_of(N_HEADS_HALF + hd, B_V), :] = out.astype(BF16)

    heads = []
    for qt in range(qa_ref.shape[0]):
        heads += [(a_scores, a_output, qt, hd) for hd in range(N_HEADS_HALF)]
        heads += [(b_scores, b_output, qt, hd) for hd in range(N_HEADS_HALF)]
    lookahead = len(s_refs) - 1
    maxes = {i: heads[i][0](heads[i][2], heads[i][3], s_refs[i % len(s_refs)]) for i in range(lookahead)}
    for i, (_, output, qt, hd) in enumerate(heads):
        j = i + lookahead
        if j < len(heads):
            maxes[j] = heads[j][0](heads[j][2], heads[j][3], s_refs[j % len(s_refs)])
        output(qt, hd, s_refs[i % len(s_refs)], maxes.pop(i))
        if i + 1 == len(heads) or heads[i + 1][2] != qt:
            rows = rows_of(qt, TILE)
            o_ref[rows, :] = h_ref[rows, :] + mod_ref[2:3, :] * _dot_tn(y_ref[qt], w_out_ref[...])


def _even_attn(h, mod, shared_mod, q, ctx_kv, lat_kv, w_out, step):
    bsz, seq, d = h.shape
    n_q = step // TILE
    qa, qb = q
    with_latent_keys = lat_kv is not None
    n_buffers = N_SCORE_BUFFERS if with_latent_keys else N_SCORE_BUFFERS_CTX
    kv = list(ctx_kv) + (list(lat_kv) if with_latent_keys else [])
    n_keys = sum(arr.shape[1] for arr in (ctx_kv[0],) + ((lat_kv[0],) if with_latent_keys else ())) * TILE

    def tile_spec(arr):
        return pl.BlockSpec((None, n_q) + arr.shape[2:], lambda b, t: (b, t, 0, 0))

    return pl.pallas_call(
        functools.partial(_even_attn_kernel, with_latent_keys, n_buffers),
        grid=(bsz, seq // step),
        in_specs=[pl.BlockSpec((None, step, d), lambda b, t: (b, t, 0)), _mod_spec(mod, shared_mod),
                  tile_spec(qa), tile_spec(qb)] + [_batch_spec(arr) for arr in kv] + [_const_spec(w_out.shape)],
        out_specs=pl.BlockSpec((None, step, d), lambda b, t: (b, t, 0)),
        out_shape=jax.ShapeDtypeStruct((bsz, seq, d), F32),
        scratch_shapes=[pltpu.VMEM((n_keys, TILE), F32)] * n_buffers
        + [pltpu.VMEM((n_q, D_MODEL, TILE), BF16)],
        compiler_params=_params(),
        name="even_attn",
    )(h, mod, qa, qb, *kv, w_out)


def _conv_ffn_kernel(final_norm, *refs):
    h_ref, prev_ref, next_ref, mod_ref, g2_ref, w_up_ref, cw_ref, cb_ref, w_down_ref = refs[:9]
    if final_norm:
        gf_ref = refs[9]
    o_ref = refs[-2 - 2 * FFN_BUFFERS]
    hs_flat = refs[-1 - 2 * FFN_BUFFERS:-1]
    act_ref = refs[-1]
    t = pl.program_id(1)
    last_tile = pl.num_programs(1) - 1
    tile = h_ref.shape[0]
    n_chunks = D_FF // FF_CHUNK
    g2 = g2_ref[...]
    shift, scale = mod_ref[3:4, :], mod_ref[4:5, :]
    has_prev = (t != 0).astype(F32)
    has_next = (t != last_tile).astype(F32)
    h = h_ref[...]
    z = jnp.concatenate([
        _norm_modulate(h, g2, shift, scale),
        _norm_modulate(prev_ref[...], g2, shift, scale) * has_prev,
        _norm_modulate(next_ref[...], g2, shift, scale) * has_next,
    ], axis=0).astype(BF16)

    hs_refs = [hs_flat[2 * i:2 * i + 2] for i in range(FFN_BUFFERS)]
    groups = tile // SUBLANES
    sub = lax.broadcasted_iota(jnp.int32, (groups, SUBLANES, FF_CHUNK), 1)

    def up(j):
        for hs_ref, col in zip(hs_refs[j % FFN_BUFFERS], (j * FF_CHUNK, D_FF + j * FF_CHUNK)):
            hs_ref[...] = _dot(z, w_up_ref[:, col:col + FF_CHUNK])

    def conv(hs_ref, col):
        hs = hs_ref[0:tile, :].reshape(groups, SUBLANES, FF_CHUNK)
        before = hs_ref[tile:tile + SUBLANES, :].reshape(1, SUBLANES, FF_CHUNK)
        after = hs_ref[tile + SUBLANES:tile + 2 * SUBLANES, :].reshape(1, SUBLANES, FF_CHUNK)
        down = pltpu.roll(jnp.concatenate([before, hs], axis=0), 1, 1)
        up_ = pltpu.roll(jnp.concatenate([hs, after], axis=0), SUBLANES - 1, 1)
        below = jnp.where(sub == 0, down[:groups], down[1:])
        above = jnp.where(sub == SUBLANES - 1, up_[1:], up_[:groups])
        cw = cw_ref[:, col:col + FF_CHUNK]
        return below * cw[0:1, :] + hs * cw[1:2, :] + above * cw[2:3, :] + cb_ref[:, col:col + FF_CHUNK]

    def gate(j):
        a = conv(hs_refs[j % FFN_BUFFERS][0], j * FF_CHUNK)
        g = conv(hs_refs[j % FFN_BUFFERS][1], D_FF + j * FF_CHUNK)
        act = (g * jax.nn.sigmoid(g) * a).reshape(tile, FF_CHUNK)
        act_ref[:, j * FF_CHUNK:(j + 1) * FF_CHUNK] = act.astype(BF16)

    lookahead = FFN_BUFFERS - 1
    for j in range(lookahead):
        up(j)
    for j in range(n_chunks):
        if j + lookahead < n_chunks:
            up(j + lookahead)
        gate(j)
    out = h + mod_ref[5:6, :] * _dot(act_ref[...], w_down_ref[...])
    if final_norm:
        ms = jnp.mean(out * out, axis=-1, keepdims=True)
        out = out * lax.rsqrt(ms + NORM_EPS) * gf_ref[...]
    o_ref[...] = out


def _conv_ffn(h, mod, shared_mod, g2, w_up, conv_w, conv_b, w_down, step, final_g=None):
    bsz, seq, d = h.shape
    rows_per_step = step // SUBLANES
    last_block = seq // SUBLANES - 1
    final_norm = final_g is not None

    def weight_spec(w):
        return pl.BlockSpec(w.shape, lambda b, t, _n=w.ndim: (0,) * _n, pipeline_mode=pl.Buffered(1))

    in_specs = [
        pl.BlockSpec((None, step, d), lambda b, t: (b, t, 0)),
        pl.BlockSpec((None, SUBLANES, d), lambda b, t: (b, jnp.maximum(t * rows_per_step - 1, 0), 0)),
        pl.BlockSpec((None, SUBLANES, d),
                     lambda b, t: (b, jnp.minimum((t + 1) * rows_per_step, last_block), 0)),
        _mod_spec(mod, shared_mod),
        _const_spec(g2.shape), weight_spec(w_up), _const_spec(conv_w.shape),
        _const_spec(conv_b.shape), weight_spec(w_down),
    ]
    args = [h, h, h, mod, g2, w_up, conv_w, conv_b, w_down]
    if final_norm:
        in_specs.append(_const_spec(final_g.shape))
        args.append(final_g)
    return pl.pallas_call(
        functools.partial(_conv_ffn_kernel, final_norm),
        grid=(bsz, seq // step),
        in_specs=in_specs,
        out_specs=pl.BlockSpec((None, step, d), lambda b, t: (b, t, 0)),
        out_shape=jax.ShapeDtypeStruct((bsz, seq, d), F32),
        scratch_shapes=[pltpu.VMEM((step + 2 * SUBLANES, FF_CHUNK), F32)] * (2 * FFN_BUFFERS)
        + [pltpu.VMEM((step, D_FF), BF16)],
        compiler_params=pltpu.CompilerParams(
            dimension_semantics=("parallel", "arbitrary"), vmem_limit_bytes=VMEM_LIMIT,
            allow_input_fusion=[a is w_up or a is w_down for a in args]),
        name="conv_ffn",
    )(*args)


def _gelu(x):
    return 0.5 * x * (1.0 + jnp.tanh(0.7978845608028654 * (x + 0.044715 * (x * x * x))))


def _odd_proj_kernel(h_ref, mod_ref, g1_ref, w_row_ref, w_t_ref, ln_g_ref, ln_b_ref, ws_ref, bs_ref,
                     cos_ht_ref, sin_ht_ref, cos_k_ref, sin_k_ref,
                     yc_ref, qd_ref, kd_ref, vd_ref):
    tile = TILE
    chunks = tile // C_CHUNK
    lane = lax.broadcasted_iota(jnp.int32, (tile, LANES), 1)
    gw = C_WIDTH // C_GROUPS
    for s in range(h_ref.shape[0] // tile):
        z = _norm_modulate(h_ref[s * tile:(s + 1) * tile, :], g1_ref[...],
                           mod_ref[0:1, :], mod_ref[1:2, :]).astype(BF16)
        ht_v = _dot_nt(w_t_ref[C_WIDTH:2 * C_WIDTH, :], z)
        ht_u = _dot_nt(w_t_ref[0:C_WIDTH, :], z)
        ht_d = _dot_nt(w_t_ref[2 * C_WIDTH:, :], z)
        hr = _dot(z, w_row_ref[...])

        v = _gelu(ht_v)
        mu = jnp.mean(v, axis=0, keepdims=True)
        vc = v - mu
        var = jnp.mean(vc * vc, axis=0, keepdims=True)
        vn = vc * lax.rsqrt(var + NORM_EPS) * _lane_tile(ln_g_ref[...], tile) + _lane_tile(ln_b_ref[...], tile)
        vn = vn.astype(BF16)
        for g in range(C_GROUPS):
            for n in range(chunks):
                cols = slice(n * C_CHUNK, (n + 1) * C_CHUNK)
                rows = slice(g * gw, (g + 1) * gw)
                mixed = _dot(vn[rows, cols], ws_ref[g]) + bs_ref[g]
                yc_ref[s, rows, cols] = (_gelu(ht_u[rows, cols]) * mixed).astype(BF16)

        qd = ht_d[0:A_Q].reshape(N_HEADS_HALF, HEAD_DIM, tile)
        qd = _rope_rows(qd, cos_ht_ref[s], sin_ht_ref[s]) * (HEAD_DIM ** -0.5 * LOG2_E)
        qd_ref[s] = qd.reshape(A_Q, tile).astype(BF16)
        kd = _rope_lanes(hr, cos_k_ref[s], sin_k_ref[s], HEAD_DIM // 2,
                         (lane & (HEAD_DIM - 1)) < HEAD_DIM // 2).astype(BF16)
        vd = ht_d[A_Q:A_Q + A_KV].astype(BF16)
        for n in range(chunks):
            kd_ref[s * chunks + n] = kd[n * C_CHUNK:(n + 1) * C_CHUNK, :]
            vd_ref[s * chunks + n] = vd[:, n * C_CHUNK:(n + 1) * C_CHUNK]


def _odd_proj(h, mod, g1, wts, tabs, step):
    bsz, seq, d = h.shape
    n_sub = step // TILE
    n_tiles = seq // TILE
    chunks = TILE // C_CHUNK

    def tiles_spec(rows):
        return pl.BlockSpec((None, n_sub, rows, TILE), lambda b, t: (b, t, 0, 0))

    chunk_spec = pl.BlockSpec((None, n_sub * chunks, C_CHUNK, C_CHUNK), lambda b, t: (b, t, 0, 0))
    in_specs = [pl.BlockSpec((None, step, d), lambda b, t: (b, t, 0)), _mod_spec(mod, False),
                _const_spec(g1.shape)]
    in_specs += [_const_spec(w.shape) for w in wts]
    in_specs += [pl.BlockSpec((n_sub,) + tb.shape[1:], lambda b, t: (t, 0, 0)) for tb in tabs]
    return pl.pallas_call(
        _odd_proj_kernel,
        grid=(bsz, seq // step),
        in_specs=in_specs,
        out_specs=[tiles_spec(C_WIDTH), tiles_spec(A_Q), chunk_spec, chunk_spec],
        out_shape=[jax.ShapeDtypeStruct((bsz, n_tiles, C_WIDTH, TILE), BF16),
                   jax.ShapeDtypeStruct((bsz, n_tiles, A_Q, TILE), BF16),
                   jax.ShapeDtypeStruct((bsz, n_tiles * chunks, C_CHUNK, A_KV), BF16),
                   jax.ShapeDtypeStruct((bsz, n_tiles * chunks, A_KV, C_CHUNK), BF16)],
        compiler_params=_params(),
        name="odd_proj",
    )(h, mod, g1, *wts, *tabs)


def _odd_kv_kernel(h_ref, mod_ref, g1_ref, w_k_ref, w_vt_ref, kd_ref, vd_ref):
    z = _norm_modulate(h_ref[...], g1_ref[...], mod_ref[0:1, :], mod_ref[1:2, :]).astype(BF16)
    kd = _dot(z, w_k_ref[...]).astype(BF16)
    vd = _dot_nt(w_vt_ref[...], z).astype(BF16)
    for n in range(h_ref.shape[0] // C_CHUNK):
        kd_ref[n] = kd[n * C_CHUNK:(n + 1) * C_CHUNK, :]
        vd_ref[n] = vd[:, n * C_CHUNK:(n + 1) * C_CHUNK]


def _odd_kv(h, mod, g1, w_k, w_vt):
    bsz, seq, d = h.shape
    chunks = seq // C_CHUNK
    chunk_spec = pl.BlockSpec((None, chunks, C_CHUNK, C_CHUNK), lambda b, t: (b, 0, 0, 0))
    return pl.pallas_call(
        _odd_kv_kernel,
        grid=(bsz, 1),
        in_specs=[pl.BlockSpec((None, seq, d), lambda b, t: (b, 0, 0)), _mod_spec(mod, True),
                  _const_spec(g1.shape), _const_spec(w_k.shape), _const_spec(w_vt.shape)],
        out_specs=[chunk_spec, chunk_spec],
        out_shape=[jax.ShapeDtypeStruct((bsz, chunks, C_CHUNK, A_KV), BF16),
                   jax.ShapeDtypeStruct((bsz, chunks, A_KV, C_CHUNK), BF16)],
        compiler_params=_params(),
        name="odd_kv",
    )(h, mod, g1, w_k, w_vt)


def _window_attn_kernel(sink_ref, h_ref, mod_ref, yc_ref, qd_ref, kdc_ref, vdc_ref, kdl_ref, vdl_ref, w_out_ref,
                        o_ref, s0_ref, s1_ref, s2_ref, s3_ref, s4_ref, s5_ref, bias_ref, y_ref):
    t = pl.program_id(1)
    tile = qd_ref.shape[1]
    ctx_chunks = kdc_ref.shape[0]
    lat_chunks = kdl_ref.shape[0]
    sub = tile // C_CHUNK
    win_chunks = sub + 2
    w0 = jnp.clip(t * sub - 1, 0, lat_chunks - win_chunks)
    n_ctx = ctx_chunks * C_CHUNK
    n_win = win_chunks * C_CHUNK
    pair = 2 * C_CHUNK
    s_refs = (s0_ref, s1_ref, s2_ref, s3_ref, s4_ref, s5_ref)

    k_ctx = kdc_ref[...].reshape(n_ctx, A_KV)
    k_win = kdl_ref[pl.ds(w0, win_chunks)].reshape(n_win, A_KV)
    keys = jnp.concatenate([k_ctx, k_win], axis=0)
    key_pos = w0 * C_CHUNK + lax.broadcasted_iota(jnp.int32, (n_win, tile), 0)
    qry_pos = t * tile + lax.broadcasted_iota(jnp.int32, (n_win, tile), 1)
    bias_ref[...] = jnp.where(jnp.abs(qry_pos - key_pos) <= WINDOW, 0.0, NEG_INF)
    zeros = jnp.zeros((HEAD_DIM, tile), BF16)
    ones = jnp.ones((BF16_ROWS, pair), BF16)

    def scores(hd):
        s_ref = s_refs[hd % len(s_refs)]
        q = qd_ref[hd * HEAD_DIM:(hd + 1) * HEAD_DIM, :]
        q = jnp.concatenate([q, zeros] if hd < GQA_GROUP else [zeros, q], axis=0)
        s = _dot(keys, q)
        s_ctx = s[:n_ctx]
        s_win = s[n_ctx:] + bias_ref[...]
        s_ref[0:n_ctx, :] = s_ctx
        s_ref[n_ctx:, :] = s_win
        m = jnp.maximum(jnp.max(s_ctx.reshape(n_ctx // SUBLANES, SUBLANES, tile), axis=0),
                        jnp.max(s_win.reshape(n_win // SUBLANES, SUBLANES, tile), axis=0))
        return jnp.maximum(jnp.max(m, axis=0, keepdims=True), sink_ref[hd] * LOG2_E)

    def output(hd, m):
        s_ref = s_refs[hd % len(s_refs)]
        rows = slice((hd // GQA_GROUP) * HEAD_DIM, (hd // GQA_GROUP + 1) * HEAD_DIM)
        acc = jnp.zeros((HEAD_DIM + BF16_ROWS, tile), F32)
        for c in range((n_ctx + n_win) // pair):
            p = jnp.exp2(s_ref[c * pair:(c + 1) * pair, :] - m)
            if 2 * c < ctx_chunks:
                v = [vdc_ref[2 * c, rows, :], vdc_ref[2 * c + 1, rows, :]]
            else:
                first = w0 + (2 * c - ctx_chunks)
                v = [vdl_ref[first, rows, :], vdl_ref[first + 1, rows, :]]
            v = jnp.concatenate([jnp.concatenate(v, axis=1), ones], axis=0)
            acc = acc + _dot(v, p.astype(BF16))
        l = acc[HEAD_DIM:HEAD_DIM + 1] + jnp.exp2(sink_ref[hd] * LOG2_E - m)
        y_ref[hd * HEAD_DIM:(hd + 1) * HEAD_DIM, :] = (acc[:HEAD_DIM] * (1.0 / l)).astype(BF16)

    lookahead = len(s_refs) - 1
    maxes = {hd: scores(hd) for hd in range(lookahead)}
    for hd in range(N_HEADS_HALF):
        if hd + lookahead < N_HEADS_HALF:
            maxes[hd + lookahead] = scores(hd + lookahead)
        output(hd, maxes.pop(hd))

    y = _dot_tn(yc_ref[...], w_out_ref[0:C_WIDTH, :]) + _dot_tn(y_ref[...], w_out_ref[C_WIDTH:, :])
    o_ref[...] = h_ref[...] + mod_ref[2:3, :] * y


def _window_attn(sink, h, mod, yc, qd, ctx_kv, lat_kv, w_out):
    bsz, seq, d = h.shape

    def tile_spec(rows):
        return pl.BlockSpec((None, None, rows, TILE), lambda b, t: (b, t, 0, 0))

    kv = list(ctx_kv) + list(lat_kv)
    return pl.pallas_call(
        _window_attn_kernel,
        grid=(bsz, seq // TILE),
        in_specs=[pl.BlockSpec(memory_space=pltpu.SMEM),
                  pl.BlockSpec((None, TILE, d), lambda b, t: (b, t, 0)), _mod_spec(mod, False),
                  tile_spec(C_WIDTH), tile_spec(A_Q)] + [_batch_spec(arr) for arr in kv]
        + [_const_spec(w_out.shape)],
        out_specs=pl.BlockSpec((None, TILE, d), lambda b, t: (b, t, 0)),
        out_shape=jax.ShapeDtypeStruct((bsz, seq, d), F32),
        scratch_shapes=[pltpu.VMEM((2 * TILE + 2 * C_CHUNK, TILE), F32)] * 6
        + [pltpu.VMEM((TILE + 2 * C_CHUNK, TILE), F32), pltpu.VMEM((A_Q, TILE), BF16)],
        compiler_params=_params(),
        name="window_attn",
    )(sink, h, mod, yc, qd, *kv, w_out)


def _rope_tables(seq):
    def base(dim):
        n_rows = seq // GRID_W
        rows = np.repeat(np.arange(n_rows), GRID_W).astype(np.float64)
        cols = np.tile(np.arange(GRID_W), n_rows).astype(np.float64)
        quarter = dim // 4
        inv_freq = ROPE_THETA ** (-np.arange(quarter, dtype=np.float64) / quarter)
        ang = np.concatenate([rows[:, None] * inv_freq, cols[:, None] * inv_freq], axis=-1)
        return np.cos(ang), np.sin(ang)

    n_tiles = seq // TILE

    def feature_major(tb):
        return jnp.asarray(tb.reshape(n_tiles, TILE, tb.shape[1]).transpose(0, 2, 1), F32)

    def token_major(tb):
        return jnp.asarray(tb.reshape(n_tiles, TILE, LANES), F32)

    cos_h, sin_h = base(HEAD_DIM)
    cos_r, sin_r = base(B_ROPE)
    cos_k = np.tile(cos_h, (1, LANES // (HEAD_DIM // 2)))
    sin_k = np.tile(np.concatenate([-sin_h, sin_h], axis=1), (1, LANES // HEAD_DIM))
    pad_l = np.zeros((seq, B_NOPE))
    pad_r = np.zeros((seq, LANES - B_QK))
    cos_kr = np.concatenate([pad_l, cos_r, cos_r, pad_r], axis=1)
    sin_kr = np.concatenate([pad_l, -sin_r, sin_r, pad_r], axis=1)
    return dict(cos_ht=feature_major(cos_h), sin_ht=feature_major(sin_h),
                cos_rt=feature_major(cos_r), sin_rt=feature_major(sin_r),
                cos_k=token_major(cos_k), sin_k=token_major(sin_k),
                cos_kr=token_major(cos_kr), sin_kr=token_major(sin_kr))


def _col(v):
    return jnp.broadcast_to(v.astype(F32)[:, None], (v.shape[0], LANES))


def _ffn_weights(w_up, conv_w, conv_b, w_down):
    return w_up.astype(BF16), conv_w, conv_b[None, :], w_down.astype(BF16)


def kernel(x, c, ctx, c_ctx, mod_w, mod_b, norm1_g, norm2_g, ev_w_in, ev_qa_g, ev_ka_g, ev_qlat_g, ev_w_q_up, ev_kvlat_g, ev_w_kv_up, ev_w_out, od_w_in, od_ln_g, od_ln_b, od_sgu_w, od_sgu_b, od_sink, od_w_out, ffn_up, ffn_conv_w, ffn_conv_b, ffn_down, final_g):
    bsz, seq, d = x.shape
    ctx_len = ctx.shape[1]
    assert d == D_MODEL and ctx_len % TILE == 0 and seq % FFN_STEP == 0 and mod_w.shape[0] == 2

    n_rows = -(-(bsz + 1) // SUBLANES) * SUBLANES
    cc = jnp.concatenate([c, jnp.zeros((n_rows - bsz - 1, d), F32), c_ctx[None, :]], axis=0)
    mods = _modulation(cc, mod_w, mod_b).reshape(2, n_rows, 6, d)
    tabs = _rope_tables(seq)

    w_in = ev_w_in[0]
    s0, s1, s2, s3, s4 = A_Q, A_Q + A_KV, A_Q + 2 * A_KV, A_Q + 2 * A_KV + B_Q_RANK, A_Q + 2 * A_KV + B_Q_RANK + B_KV_RANK
    w_kr = jnp.zeros((d, LANES), F32).at[:, B_NOPE:B_QK].set(w_in[:, s4:])
    w_row = jnp.concatenate([w_in[:, s0:s1], w_in[:, s3:s4], w_kr], axis=1).astype(BF16)
    w_t = jnp.concatenate([w_in[:, :s0], w_in[:, s1:s2], w_in[:, s2:s3], w_in[:, s3:s4]], axis=1).T.astype(BF16)
    wq = ev_w_q_up[0].reshape(B_Q_RANK, N_HEADS_HALF, B_QK)
    wq = jnp.pad(wq, ((0, 0), (0, 0), (0, LANES - B_QK))).reshape(B_Q_RANK, N_HEADS_HALF * LANES).T.astype(BF16)
    wkv = ev_w_kv_up[0].reshape(B_KV_RANK, N_HEADS_HALF, B_NOPE + B_V)
    wk = jnp.pad(wkv[:, :, :B_NOPE], ((0, 0), (0, 0), (0, LANES - B_NOPE)))
    wk = wk.reshape(B_KV_RANK, N_HEADS_HALF * LANES).astype(BF16)
    wv = wkv[:, :, B_NOPE:].reshape(B_KV_RANK, N_HEADS_HALF * B_V).T.astype(BF16)
    even_wts = [w_row, w_t, wq, wk, wv,
                _col(ev_qa_g[0] * (HEAD_DIM ** -0.5 * LOG2_E)), jnp.tile(ev_ka_g[0], KV_HEADS)[None, :],
                _col(ev_qlat_g[0]), _col(ev_kvlat_g[0]), ev_kvlat_g[0][None, :]]
    even_tabs = [tabs[k] for k in ("cos_ht", "sin_ht", "cos_rt", "sin_rt", "cos_k", "sin_k", "cos_kr", "sin_kr")]
    g1 = norm1_g[0][None, :]
    qa_c, ka_c, va_c, qb_c, kb_c, vb_c = _even_proj(ctx, mods[0], True, g1, even_wts, None, ctx_len)
    qa_l, ka_l, va_l, qb_l, kb_l, vb_l = _even_proj(x, mods[0], False, g1, even_wts, even_tabs, LATENT_STEP)
    ctx_kv = (ka_c, va_c, kb_c, vb_c)
    w_out = ev_w_out[0].astype(BF16)
    hc = _even_attn(ctx, mods[0], True, (qa_c, qb_c), ctx_kv, None, w_out, TILE)
    hl = _even_attn(x, mods[0], False, (qa_l, qb_l), ctx_kv, (ka_l, va_l, kb_l, vb_l), w_out, TILE)
    ffn0 = _ffn_weights(ffn_up[0], ffn_conv_w[0], ffn_conv_b[0], ffn_down[0])
    g2 = norm2_g[0][None, :]
    hc = _conv_ffn(hc, mods[0], True, g2, *ffn0, step=ctx_len)
    hl = _conv_ffn(hl, mods[0], False, g2, *ffn0, step=FFN_STEP)

    w_in = od_w_in[0]
    o2, o3 = 2 * C_WIDTH + A_Q, 2 * C_WIDTH + A_Q + A_KV
    w_row = w_in[:, o2:o3].astype(BF16)
    w_t = jnp.concatenate([w_in[:, :o2], w_in[:, o3:]], axis=1).T.astype(BF16)
    ws_t = od_sgu_w[0].transpose(0, 2, 1).astype(BF16)
    odd_wts = [w_row, w_t, _col(od_ln_g[0]), _col(od_ln_b[0]), ws_t, od_sgu_b[0][:, None, :]]
    odd_tabs = [tabs[k] for k in ("cos_ht", "sin_ht", "cos_k", "sin_k")]
    g1 = norm1_g[1][None, :]
    kd_c, vd_c = _odd_kv(hc, mods[1], g1, w_row, w_in[:, o3:].T.astype(BF16))
    yc, qd, kd_l, vd_l = _odd_proj(hl, mods[1], g1, odd_wts, odd_tabs, LATENT_STEP)
    hl = _window_attn(od_sink[0], hl, mods[1], yc, qd, (kd_c, vd_c), (kd_l, vd_l), od_w_out[0].astype(BF16))
    return _conv_ffn(hl, mods[1], False, norm2_g[1][None, :],
                     *_ffn_weights(ffn_up[1], ffn_conv_w[1], ffn_conv_b[1], ffn_down[1]),
                     step=FFN_STEP, final_g=final_g[None, :])
```
